```python
import math
import jax, jax.numpy as jnp
from jax import lax
import numpy as np

D_MODEL = 1024
BATCH = 8
SEQ = 4096
DEPTH = 2

MEM_LEN = 256
NSA_HEADS = 8
NSA_GROUPS = 2
NSA_HPG = NSA_HEADS // NSA_GROUPS
NSA_DH = D_MODEL // 16
NSA_WIDTH = NSA_HEADS * NSA_DH
KV_WIDTH = NSA_GROUPS * NSA_DH
CMP_LEN = 32
CMP_STRIDE = 16
SEL_BLOCK = 64
SEL_TOP = 16
WINDOW = 512
NSA_QBLOCK = 64
CONV_WIDTH = D_MODEL // 2
CONV_K = 31
GLA_HEADS = 4
GLA_DK = D_MODEL // 16
GLA_DV = D_MODEL // 8
GLA_RANK = 16
GLA_TAU = 16.0
GLA_CHUNK = 64
REL_BUCKETS = 32
REL_MAX_DIST = 128
XA_HEADS = 4
XA_DH = D_MODEL // XA_HEADS
N_EXPERTS = 32
TOP_K = 4
D_EXPERT = D_MODEL
SWIGLU_ALPHA = 1.702
SWIGLU_LIMIT = 7.0
MOE_BLOCK = 128
N_BRANCH = 3
BRANCH_WIDTH = D_MODEL // 2
DEEPNORM_ALPHA = (2 * DEPTH) ** 0.25
DEEPNORM_BETA = (8 * DEPTH) ** -0.25
IN_SIZES = (NSA_WIDTH, 6 * KV_WIDTH, 3 * NSA_HEADS, 2 * CONV_WIDTH,
            GLA_HEADS * GLA_DK, GLA_HEADS * GLA_DK, GLA_HEADS * GLA_DV, GLA_RANK,
            GLA_HEADS * GLA_DV, N_BRANCH * D_MODEL)
D_IN = sum(IN_SIZES)
IN_OFFSETS = tuple(int(o) for o in np.cumsum(IN_SIZES)[:-1])

kernel_name = 'hybrid_nsa_conv_gla_moe_deepnorm'


def layer_norm(x, g, b, eps=1e-5):
    xf = x.astype(jnp.float32)
    mu = jnp.mean(xf, -1, keepdims=True)
    var = jnp.mean(jnp.square(xf - mu), -1, keepdims=True)
    return ((xf - mu) * lax.rsqrt(var + eps) * g + b).astype(x.dtype)


def rms_norm(x, g, eps=1e-6):
    xf = x.astype(jnp.float32)
    return (xf * lax.rsqrt(jnp.mean(jnp.square(xf), -1, keepdims=True) + eps) * g).astype(x.dtype)


def masked_softmax(s, mask):
    s = jnp.where(mask, s, -jnp.inf)
    m = jnp.max(s, -1, keepdims=True)
    m = jnp.where(jnp.isfinite(m), m, 0.0)
    p = jnp.exp(s - m)
    return p / jnp.maximum(jnp.sum(p, -1, keepdims=True), 1e-30)


def t5_bucket(dist):
    n = jnp.maximum(dist, 0)
    exact = REL_BUCKETS // 2
    log_ratio = jnp.log(jnp.maximum(n, 1).astype(jnp.float32) / exact) / math.log(REL_MAX_DIST / exact)
    large = exact + (log_ratio * (REL_BUCKETS - exact)).astype(jnp.int32)
    return jnp.where(n < exact, n, jnp.minimum(large, REL_BUCKETS - 1))


def compress_blocks(kv, pe, w1, b1, w2):
    S = kv.shape[1]
    n_cmp = (S - CMP_LEN) // CMP_STRIDE + 1
    idx = jnp.arange(n_cmp)[:, None] * CMP_STRIDE + jnp.arange(CMP_LEN)[None, :]
    blocks = kv[:, idx] + pe[None, None, :, None, :]
    h = jax.nn.gelu(jnp.einsum('bclgd,lde->bcge', blocks, w1) + b1)
    return h @ w2


def nsa_attention(q, k_cmp, v_cmp, k_sel, v_sel, k_win, v_win, gates, rel_bias):
    B, S, G, HG, DH = q.shape
    n_cmp = k_cmp.shape[1]
    n_slc = S // SEL_BLOCK
    n_sel = min(SEL_TOP, n_slc)
    n_tok = n_sel * SEL_BLOCK
    span = NSA_QBLOCK + WINDOW
    cmp_start = jnp.arange(n_cmp) * CMP_STRIDE
    cmp_end = cmp_start + CMP_LEN - 1
    blk = jnp.arange(n_slc)
    overlap = ((cmp_start[:, None] < (blk[None, :] + 1) * SEL_BLOCK)
               & (cmp_start[:, None] + CMP_LEN > blk[None, :] * SEL_BLOCK)).astype(jnp.float32)
    ks_blocks = k_sel.reshape(B, n_slc, SEL_BLOCK, G, DH).transpose(0, 3, 1, 2, 4)
    vs_blocks = v_sel.reshape(B, n_slc, SEL_BLOCK, G, DH).transpose(0, 3, 1, 2, 4)
    pad = ((0, 0), (WINDOW, 0), (0, 0), (0, 0))
    kw_pad = jnp.pad(k_win, pad)
    vw_pad = jnp.pad(v_win, pad)
    bias_hg = rel_bias.reshape(REL_BUCKETS, G, HG)
    gather_blocks = jax.vmap(jax.vmap(lambda blocks, ix: blocks[ix]))
    g_idx = jnp.arange(G)[None, :, None, None]

    def one_block(i):
        q0 = i * NSA_QBLOCK
        t = q0 + jnp.arange(NSA_QBLOCK)
        qb = lax.dynamic_slice_in_dim(q, q0, NSA_QBLOCK, axis=1)
        gb = lax.dynamic_slice_in_dim(gates, q0, NSA_QBLOCK, axis=1)
        dist_c = t[:, None] - cmp_end[None, :]
        s_c = (jnp.einsum('bqghd,bcgd->bqghc', qb, k_cmp).astype(jnp.float32)
               + bias_hg[t5_bucket(dist_c)].transpose(0, 2, 3, 1))
        p_c = masked_softmax(s_c, (dist_c >= 0)[:, None, None, :])
        o_c = jnp.einsum('bqghc,bcgd->bqghd', p_c.astype(v_cmp.dtype), v_cmp)
        cur = t // SEL_BLOCK
        imp = jnp.einsum('bqghc,cj->bqgj', p_c, overlap)
        future = blk[None, :] > cur[:, None]
        forced = (blk[None, :] == 0) | (blk[None, :] == cur[:, None]) | (blk[None, :] == cur[:, None] - 1)
        imp = jnp.where(forced[:, None, :], jnp.inf, jnp.where(future[:, None, :], -jnp.inf, imp))
        _, idx = lax.top_k(imp, n_sel)
        idx = idx.transpose(0, 2, 1, 3)
        k_g = gather_blocks(ks_blocks, idx).reshape(B, G, NSA_QBLOCK, n_tok, DH)
        v_g = gather_blocks(vs_blocks, idx).reshape(B, G, NSA_QBLOCK, n_tok, DH)
        pos = (idx[..., None] * SEL_BLOCK + jnp.arange(SEL_BLOCK)).reshape(B, G, NSA_QBLOCK, n_tok)
        dist_s = t[:, None] - pos
        s_s = (jnp.einsum('bqghd,bgqnd->bgqhn', qb, k_g).astype(jnp.float32)
               + bias_hg[t5_bucket(dist_s), g_idx].transpose(0, 1, 2, 4, 3))
        p_s = masked_softmax(s_s, (dist_s >= 0)[:, :, :, None, :])
        o_s = jnp.einsum('bgqhn,bgqnd->bqghd', p_s.astype(v_g.dtype), v_g)
        kw = lax.dynamic_slice_in_dim(kw_pad, q0, span, axis=1)
        vw = lax.dynamic_slice_in_dim(vw_pad, q0, span, axis=1)
        key_pos = q0 - WINDOW + jnp.arange(span)
        dist_w = t[:, None] - key_pos[None, :]
        mask_w = (dist_w >= 0) & (dist_w < WINDOW) & (key_pos[None, :] >= 0)
        s_w = (jnp.einsum('bqghd,blgd->bqghl', qb, kw).astype(jnp.float32)
               + bias_hg[t5_bucket(dist_w)].transpose(0, 2, 3, 1))
        p_w = masked_softmax(s_w, mask_w[:, None, None, :])
        o_w = jnp.einsum('bqghl,blgd->bqghd', p_w.astype(vw.dtype), vw)
        o = gb[..., 0:1] * o_c + gb[..., 1:2] * o_s + gb[..., 2:3] * o_w
        return o.reshape(B, NSA_QBLOCK, G * HG * DH)

    out = lax.map(one_block, jnp.arange(S // NSA_QBLOCK))
    return out.transpose(1, 0, 2, 3).reshape(B, S, G * HG * DH)


def conformer_conv(u_pair, w, b, g, beta):
    a, gate = jnp.split(u_pair, 2, axis=-1)
    u = a * jax.nn.sigmoid(gate)
    y = lax.conv_general_dilated(u, w[:, None, :], window_strides=(1,), padding=[(CONV_K - 1, 0)],
                                 dimension_numbers=('NWC', 'WIO', 'NWC'),
                                 feature_group_count=u.shape[-1]) + b
    return jax.nn.silu(layer_norm(y, g, beta))


def gla_chunked(q, k, v, log_a):
    B, S, H, DK = q.shape
    DV = v.shape[-1]
    C = GLA_CHUNK
    n = S // C

    def chunks(a):
        return a.astype(jnp.float32).reshape(B, n, C, H, a.shape[-1]).transpose(1, 0, 3, 2, 4)

    qc, kc, vc, ac = chunks(q), chunks(k), chunks(v), chunks(log_a)
    bc = jnp.cumsum(ac, axis=-2)
    causal = jnp.tril(jnp.ones((C, C), bool))[:, :, None]

    def step(state, inp):
        qi, ki, vi, bi = inp
        diff = bi[:, :, :, None, :] - bi[:, :, None, :, :]
        decay = jnp.exp(jnp.where(causal, diff, -jnp.inf))
        attn = jnp.einsum('bhtd,bhtsd->bhts', qi, decay * ki[:, :, None, :, :])
        o = attn @ vi + jnp.einsum('bhtd,bhde->bhte', qi * jnp.exp(bi), state)
        b_last = bi[:, :, -1:, :]
        state = (jnp.exp(b_last[:, :, 0, :, None]) * state
                 + jnp.einsum('bhsd,bhse->bhde', ki * jnp.exp(b_last - bi), vi))
        return state, o

    _, o = lax.scan(step, jnp.zeros((B, H, DK, DV), jnp.float32), (qc, kc, vc, bc))
    return o.transpose(1, 0, 3, 2, 4).reshape(B, S, H, DV)


def token_mixer(x, rel_bias, w_in, cmp_pe, cmp_w1, cmp_b1, cmp_w2, conv_w, conv_b, conv_g, conv_beta,
                gla_w, gla_b, gla_g, w_branch, w_out):
    B, S, D = x.shape
    z = x @ w_in
    q, kv, nsa_g, conv_in, gq, gk, gv, ga, gr, merge_g = jnp.split(z, IN_OFFSETS, axis=-1)
    q = q.reshape(B, S, NSA_GROUPS, NSA_HPG, NSA_DH) * NSA_DH ** -0.5
    kv = kv.reshape(B, S, 6, NSA_GROUPS, NSA_DH)
    k_cmp = compress_blocks(kv[:, :, 0], cmp_pe[0], cmp_w1[0], cmp_b1[0], cmp_w2[0])
    v_cmp = compress_blocks(kv[:, :, 1], cmp_pe[1], cmp_w1[1], cmp_b1[1], cmp_w2[1])
    nsa_gates = jax.nn.sigmoid(nsa_g).reshape(B, S, NSA_GROUPS, NSA_HPG, 3)
    out_a = nsa_attention(q, k_cmp, v_cmp, kv[:, :, 2], kv[:, :, 3], kv[:, :, 4], kv[:, :, 5],
                          nsa_gates, rel_bias)
    out_b = conformer_conv(conv_in, conv_w, conv_b, conv_g, conv_beta)
    log_a = jax.nn.log_sigmoid((ga @ gla_w + gla_b).astype(jnp.float32)) / GLA_TAU
    o = gla_chunked(gq.reshape(B, S, GLA_HEADS, GLA_DK) * GLA_DK ** -0.5,
                    gk.reshape(B, S, GLA_HEADS, GLA_DK),
                    gv.reshape(B, S, GLA_HEADS, GLA_DV),
                    log_a.reshape(B, S, GLA_HEADS, GLA_DK))
    out_c = rms_norm(o, gla_g).astype(x.dtype).reshape(B, S, GLA_HEADS * GLA_DV) * jax.nn.silu(gr)
    gates = jax.nn.sigmoid(merge_g).reshape(B, S, N_BRANCH, D)
    merged = (gates[:, :, 0] * (out_a @ w_branch[0])
              + gates[:, :, 1] * (out_b @ w_branch[1])
              + gates[:, :, 2] * (out_c @ w_branch[2]))
    return merged @ w_out


def cross_attention(x, mem, wq, wkv, wo):
    B, S, D = x.shape
    M = mem.shape[1]
    q = (x @ wq).reshape(B, S, XA_HEADS, XA_DH) * XA_DH ** -0.5
    kv = (mem @ wkv).reshape(B, M, 2, XA_HEADS, XA_DH)
    s = jnp.einsum('bshd,bmhd->bhsm', q, kv[:, :, 0]).astype(jnp.float32)
    p = jax.nn.softmax(s, axis=-1).astype(x.dtype)
    o = jnp.einsum('bhsm,bmhd->bshd', p, kv[:, :, 1]).reshape(B, S, D)
    return o @ wo


def moe_ffn(x, router_w, router_b, w_gu, b_gu, w_dn, b_dn):
    B, S, D = x.shape
    N = B * S
    A = N * TOP_K
    x2 = x.reshape(N, D)
    logits = (x2 @ router_w + router_b).astype(jnp.float32)
    top_v, top_i = lax.top_k(logits, TOP_K)
    gate = jax.nn.softmax(top_v, axis=-1).astype(x.dtype)
    e_flat = top_i.reshape(A)
    tok = jnp.arange(A, dtype=jnp.int32) // TOP_K
    g_flat = gate.reshape(A)
    order = jnp.argsort(e_flat)
    e_s, tok_s, g_s = e_flat[order], tok[order], g_flat[order]
    counts = jnp.bincount(e_flat, length=N_EXPERTS)
    starts = jnp.cumsum(counts) - counts
    padded = (counts + MOE_BLOCK - 1) // MOE_BLOCK * MOE_BLOCK
    pends = jnp.cumsum(padded)
    pstarts = pends - padded
    dest = pstarts[e_s] + jnp.arange(A, dtype=jnp.int32) - starts[e_s]
    n_slots = -(-(A + N_EXPERTS * MOE_BLOCK) // MOE_BLOCK) * MOE_BLOCK
    slot_tok = jnp.zeros((n_slots,), jnp.int32).at[dest].set(tok_s)
    slot_gate = jnp.zeros((n_slots,), x.dtype).at[dest].set(g_s)
    n_blocks = n_slots // MOE_BLOCK
    blk_e = jnp.minimum(jnp.searchsorted(pends, jnp.arange(n_blocks, dtype=jnp.int32) * MOE_BLOCK,
                                         side='right'), N_EXPERTS - 1)
    xs = x2[slot_tok].reshape(n_blocks, MOE_BLOCK, D)

    def expert_block(args):
        xb, gb, e = args
        gu = xb @ w_gu[e] + b_gu[e]
        g, u = gu[:, :D_EXPERT], gu[:, D_EXPERT:]
        g = jnp.minimum(g, SWIGLU_LIMIT)
        u = jnp.clip(u, -SWIGLU_LIMIT, SWIGLU_LIMIT)
        h = (u + 1.0) * g * jax.nn.sigmoid(SWIGLU_ALPHA * g)
        return (h @ w_dn[e] + b_dn[e]) * gb[:, None]

    ys = lax.map(expert_block, (xs, slot_gate.reshape(n_blocks, MOE_BLOCK), blk_e)).reshape(n_slots, D)
    return jax.ops.segment_sum(ys, slot_tok, num_segments=N).astype(x.dtype).reshape(B, S, D)


def setup_inputs(seed: int = 0) -> dict:
    key = jax.random.key(seed)
    keys = jax.random.split(key, 28)
    L, D = DEPTH, D_MODEL

    def nrm(i, shape, scale):
        return scale * jax.random.normal(keys[i], shape, jnp.float32)

    return {
        'x': nrm(0, (BATCH, SEQ, D), 1.0),
        'mem': nrm(1, (BATCH, MEM_LEN, D), 1.0),
        'rel_bias': nrm(2, (REL_BUCKETS, NSA_HEADS), 0.1),
        'w_in': nrm(3, (L, D, D_IN), D ** -0.5),
        'cmp_pe': nrm(4, (L, 2, CMP_LEN, NSA_DH), 0.1),
        'cmp_w1': nrm(5, (L, 2, CMP_LEN, NSA_DH, NSA_DH), (CMP_LEN * NSA_DH) ** -0.5),
        'cmp_b1': nrm(6, (L, 2, NSA_DH), 0.01),
        'cmp_w2': nrm(7, (L, 2, NSA_DH, NSA_DH), NSA_DH ** -0.5),
        'conv_w': nrm(8, (L, CONV_K, CONV_WIDTH), CONV_K ** -0.5),
        'conv_b': nrm(9, (L, CONV_WIDTH), 0.01),
        'conv_norm_g': 1.0 + nrm(10, (L, CONV_WIDTH), 0.01),
        'conv_norm_b': nrm(11, (L, CONV_WIDTH), 0.01),
        'gla_gate_w': nrm(12, (L, GLA_RANK, GLA_HEADS * GLA_DK), GLA_RANK ** -0.5),
        'gla_gate_b': nrm(13, (L, GLA_HEADS * GLA_DK), 0.1),
        'gla_norm_g': 1.0 + nrm(14, (L, GLA_DV), 0.01),
        'w_branch': nrm(15, (L, N_BRANCH, BRANCH_WIDTH, D), BRANCH_WIDTH ** -0.5 * DEEPNORM_BETA),
        'w_out': nrm(16, (L, D, D), D ** -0.5 * DEEPNORM_BETA),
        'xa_wq': nrm(17, (L, D, D), D ** -0.5),
        'xa_wkv': nrm(18, (L, D, 2 * D), D ** -0.5),
        'xa_wo': nrm(19, (L, D, D), D ** -0.5 * DEEPNORM_BETA),
        'router_w': nrm(20, (L, D, N_EXPERTS), D ** -0.5),
        'router_b': nrm(21, (L, N_EXPERTS), 0.01),
        'expert_w_gu': nrm(22, (L, N_EXPERTS, D, 2 * D_EXPERT), D ** -0.5),
        'expert_b_gu': nrm(23, (L, N_EXPERTS, 2 * D_EXPERT), 0.01),
        'expert_w_down': nrm(24, (L, N_EXPERTS, D_EXPERT, D), D_EXPERT ** -0.5 * DEEPNORM_BETA),
        'expert_b_down': nrm(25, (L, N_EXPERTS, D), 0.01),
        'norm_g': 1.0 + nrm(26, (L, 3, D), 0.01),
        'norm_b': nrm(27, (L, 3, D), 0.01),
    }


def reference(x, mem, rel_bias, w_in, cmp_pe, cmp_w1, cmp_b1, cmp_w2, conv_w, conv_b, conv_norm_g,
              conv_norm_b, gla_gate_w, gla_gate_b, gla_norm_g, w_branch, w_out, xa_wq, xa_wkv, xa_wo,
              router_w, router_b, expert_w_gu, expert_b_gu, expert_w_down, expert_b_down, norm_g, norm_b):
    for l in range(DEPTH):
        mix = token_mixer(x, rel_bias, w_in[l], cmp_pe[l], cmp_w1[l], cmp_b1[l], cmp_w2[l], conv_w[l],
                          conv_b[l], conv_norm_g[l], conv_norm_b[l], gla_gate_w[l], gla_gate_b[l],
                          gla_norm_g[l], w_branch[l], w_out[l])
        x = layer_norm(DEEPNORM_ALPHA * x + mix, norm_g[l, 0], norm_b[l, 0])
        xa = cross_attention(x, mem, xa_wq[l], xa_wkv[l], xa_wo[l])
        x = layer_norm(DEEPNORM_ALPHA * x + xa, norm_g[l, 1], norm_b[l, 1])
        ff = moe_ffn(x, router_w[l], router_b[l], expert_w_gu[l], expert_b_gu[l], expert_w_down[l],
                     expert_b_down[l])
        x = layer_norm(DEEPNORM_ALPHA * x + ff, norm_g[l, 2], norm_b[l, 2])
    return x
```

```python
import functools
import math

import numpy as np
import jax
import jax.numpy as jnp
from jax import lax
from jax.experimental import pallas as pl
from jax.experimental.pallas import tpu as pltpu

F32 = jnp.float32
_MM = jnp.bfloat16
_ACT = jnp.bfloat16
NEG = -1e30

D_MODEL = 1024
DEPTH = 2
MEM_LEN = 256
NSA_HEADS = 8
NSA_DH = 64
CMP_LEN = 32
CMP_STRIDE = 16
SEL_BLOCK = 64
SEL_TOP = 16
WINDOW = 512
CONV_WIDTH = 512
CONV_K = 31
GLA_HEADS = 4
GLA_DK = 64
GLA_DV = 128
GLA_RANK = 16
GLA_TAU = 16.0
GLA_CHUNK = 64
REL_BUCKETS = 32
REL_MAX_DIST = 128
XA_HEADS = 4
XA_DH = 256
N_EXPERTS = 32
TOP_K = 4
SWIGLU_ALPHA = 1.702
SWIGLU_LIMIT = 7.0
DEEPNORM_ALPHA = (2 * DEPTH) ** 0.25

Z_MERGE = 0
Z_CONV = 3072
Z_Q = 4096
Z_GV = 4608
Z_GR = 5120
Z_KV = 5632
Z_GQ = 6400
Z_GK = 6656
Z_SMALL = 6912
ZW = 7040
SMALL_GA = 24

_IN_SIZES = (512, 768, 24, 1024, 256, 256, 512, 16, 512, 3072)
_IN_OFF = np.concatenate([[0], np.cumsum(_IN_SIZES)]).astype(int)

VMEM_LIMIT = 56 * 1024 * 1024

NSA_QB = 128
MOE_BLK = 512
COMB_T = 256


def _cp(*sem):
    return pltpu.CompilerParams(dimension_semantics=sem, vmem_limit_bytes=VMEM_LIMIT)


def _dot(a, b):
    return jnp.dot(a.astype(_MM), b.astype(_MM), preferred_element_type=F32)


def _dot_nt(a, b):
    return lax.dot_general(a.astype(_MM), b.astype(_MM), (((1,), (1,)), ((), ())), preferred_element_type=F32)


def _dot_tn(a, b):
    return lax.dot_general(a.astype(_MM), b.astype(_MM), (((0,), (0,)), ((), ())), preferred_element_type=F32)


def _split3(x):
    x1 = x.astype(_MM)
    r1 = x - x1.astype(F32)
    x2 = r1.astype(_MM)
    x3 = (r1 - x2.astype(F32)).astype(_MM)
    return x1, x2, x3


def _dot01_left(m01, x):
    x1, x2, x3 = _split3(x)
    return (jnp.dot(m01, x1, preferred_element_type=F32) + jnp.dot(m01, x2, preferred_element_type=F32)
            + jnp.dot(m01, x3, preferred_element_type=F32))


def _layer_norm(y, g, b, eps=1e-5):
    mu = jnp.mean(y, axis=-1, keepdims=True)
    d = y - mu
    var = jnp.mean(d * d, axis=-1, keepdims=True)
    return d * lax.rsqrt(var + eps) * g + b


def _sigmoid(x):
    return 1.0 / (1.0 + jnp.exp(-x))


def _mm_kernel(x_ref, w_ref, o_ref):
    o_ref[...] = _dot(x_ref[...], w_ref[...]).astype(o_ref.dtype)


def _matmul(x, w, tm, tn, out_dtype):
    m, k = x.shape
    n = w.shape[1]
    tm = min(tm, m)
    return pl.pallas_call(
        _mm_kernel,
        grid=(m // tm, n // tn),
        in_specs=[pl.BlockSpec((tm, k), lambda i, j: (i, 0)), pl.BlockSpec((k, tn), lambda i, j: (0, j))],
        out_specs=pl.BlockSpec((tm, tn), lambda i, j: (i, j)),
        out_shape=jax.ShapeDtypeStruct((m, n), out_dtype),
        compiler_params=_cp("parallel", "parallel"),
        name="matmul",
    )(x, w)


def _reorder_w_in(w_in):
    def cols(i):
        return w_in[:, _IN_OFF[i]:_IN_OFF[i + 1]]
    small = jnp.concatenate([cols(2), cols(7), jnp.zeros((D_MODEL, 128 - 40), w_in.dtype)], axis=1)
    w = jnp.concatenate([cols(9), cols(3), cols(0), cols(6), cols(8), cols(1), cols(4), cols(5), small], axis=1)
    return w.astype(_MM)


def _cmp_kernel(x_ref, pea_ref, peb_ref, wa_ref, wb_ref, b1_ref, w2_ref, o_ref):
    x = x_ref[...].astype(F32)
    a = _dot(x + pea_ref[...], wa_ref[...])
    bm = _dot(x + peb_ref[...], wb_ref[...])
    n = bm.shape[0]
    bs = pltpu.roll(bm, n - 1, axis=0)
    h = jax.nn.gelu(a + bs + b1_ref[...], approximate=True)
    o_ref[...] = _dot(h, w2_ref[...]).astype(o_ref.dtype)


def _compress(kv2, cmp_pe, cmp_w1, cmp_b1, cmp_w2):
    _, B, R, _ = kv2.shape
    eye2 = jnp.eye(2, dtype=F32)

    def half(w1h):
        w = jnp.einsum('rlde,gh->rlgdhe', w1h, eye2)
        return w.reshape(2, 16 * 128, 128).astype(_MM)

    wa = half(cmp_w1[:, :16])
    wb = half(cmp_w1[:, 16:])

    def pe_half(p):
        return jnp.broadcast_to(p[:, :, None, :], (2, 16, 2, 64)).reshape(2, 1, 2048).astype(F32)

    pea = pe_half(cmp_pe[:, :16])
    peb = pe_half(cmp_pe[:, 16:])
    b1 = jnp.tile(cmp_b1, (1, 2)).reshape(2, 1, 128).astype(F32)
    w2 = jnp.einsum('rde,gh->rgdhe', cmp_w2, eye2).reshape(2, 128, 128).astype(_MM)
    return pl.pallas_call(
        _cmp_kernel,
        grid=(2, B),
        in_specs=[
            pl.BlockSpec((None, None, R, 2048), lambda r, b: (r, b, 0, 0)),
            pl.BlockSpec((None, 1, 2048), lambda r, b: (r, 0, 0)),
            pl.BlockSpec((None, 1, 2048), lambda r, b: (r, 0, 0)),
            pl.BlockSpec((None, 2048, 128), lambda r, b: (r, 0, 0)),
            pl.BlockSpec((None, 2048, 128), lambda r, b: (r, 0, 0)),
            pl.BlockSpec((None, 1, 128), lambda r, b: (r, 0, 0)),
            pl.BlockSpec((None, 128, 128), lambda r, b: (r, 0, 0)),
        ],
        out_specs=pl.BlockSpec((None, None, R, 128), lambda r, b: (r, b, 0, 0)),
        out_shape=jax.ShapeDtypeStruct((2, B, R, 128), _ACT),
        compiler_params=_cp("parallel", "parallel"),
        name="nsa_compress",
    )(kv2, pea, peb, wa, wb, b1, w2)


def _t5_bucket_np(dist):
    n = np.maximum(dist, 0)
    exact = REL_BUCKETS // 2
    lr = np.log(np.maximum(n, 1).astype(np.float32) / np.float32(exact)) / np.float32(math.log(REL_MAX_DIST / exact))
    large = exact + (lr * np.float32(REL_BUCKETS - exact)).astype(np.int32)
    return np.where(n < exact, n, np.minimum(large, REL_BUCKETS - 1))


def _nsa_tables(rel_bias):
    rel = (rel_bias - rel_bias[REL_BUCKETS - 1:REL_BUCKETS]).astype(F32)
    q = np.arange(NSA_QB)

    def table(dist, valid, fill):
        t = rel[_t5_bucket_np(dist)]
        t = jnp.where(valid[:, :, None], t, fill)
        return jnp.transpose(t, (0, 2, 1)).reshape(dist.shape[0], NSA_HEADS * NSA_QB)

    cc = np.arange(24) - 16
    d_c = q[None, :] - CMP_STRIDE * cc[:, None] - (CMP_LEN - 1)
    t_cmp = table(d_c, d_c >= 0, 0.0)
    ko = np.arange(256)
    d_n = q[None, :] + 128 - ko[:, None]
    t_near = table(d_n, d_n >= 0, NEG)
    kw = np.arange(WINDOW + NSA_QB)
    d_w = q[None, :] + WINDOW - kw[:, None]
    t_win = table(d_w, (d_w >= 0) & (d_w < WINDOW), NEG)
    return t_cmp, t_near, t_win


def _nsa_kernel(q_ref, gate_ref, kcmp_ref, vcmp_ref, ksel_ref, vsel_ref, kwin_ref, vwin_ref,
                cmpa_ref, tcmp_ref, tnear_ref, twin_ref, ovl_ref, out_ref, s_scr, sel_scr, far_scr):
    i = pl.program_id(1)
    ncmp = kcmp_ref.shape[0]
    nblk = ovl_ref.shape[0]
    qt = (q_ref[...].astype(F32) * (NSA_DH ** -0.5)).T
    gt = _sigmoid(gate_ref[...].astype(F32)).T
    zeros64 = jnp.zeros((64, NSA_QB), F32)
    jidx = lax.broadcasted_iota(jnp.int32, (nblk, NSA_QB), 0)
    qidx = lax.broadcasted_iota(jnp.int32, (nblk, NSA_QB), 1)
    cur = 2 * i + (qidx >= SEL_BLOCK).astype(jnp.int32)
    forced = (jidx == 0) | (jidx == cur) | (jidx == cur - 1)
    future = jidx > cur
    sel_scr[0:8, :] = jnp.zeros((8, NSA_QB), F32)
    far_scr[0:8, :] = jnp.zeros((8, NSA_QB), F32)
    s_scr[0:16, :] = jnp.zeros((16, 4 * NSA_QB), F32)

    def sel_rows(scr, first_block):
        rows = [jnp.broadcast_to(scr[pl.ds(8 + first_block + b, 1), :], (SEL_BLOCK, NSA_QB)) for b in range(4)]
        m = jnp.concatenate(rows, axis=0)
        return jnp.concatenate([m, m, m, m], axis=1) > 0.5

    out_rows = []
    for g in range(2):
        blocks = []
        for h in range(4):
            r = qt[64 * (4 * g + h):64 * (4 * g + h) + 64]
            blocks.append(jnp.concatenate([r, zeros64] if g == 0 else [zeros64, r], axis=0))
        q_g = jnp.concatenate(blocks, axis=1).astype(_MM)
        cs = slice(512 * g, 512 * g + 512)

        s_scr[16:16 + ncmp, :] = _dot(kcmp_ref[...], q_g)
        w0 = pl.multiple_of(8 * i, 8)
        s_scr[pl.ds(w0, 24), :] = s_scr[pl.ds(w0, 24), :] + tcmp_ref[:, cs]
        valid = cmpa_ref[...] <= NSA_QB * i
        sc = jnp.where(valid, s_scr[16:16 + ncmp, :], NEG)
        m = jnp.max(sc, axis=0, keepdims=True)
        e = jnp.where(valid, jnp.exp(sc - m), 0.0)
        l = jnp.sum(e, axis=0, keepdims=True)
        pc = e / jnp.maximum(l, 1e-30)
        o_c = _dot_tn(vcmp_ref[...], pc)

        psum = pc[:, 0:128] + pc[:, 128:256] + pc[:, 256:384] + pc[:, 384:512]
        imp = _dot01_left(ovl_ref[...], psum)
        v = jnp.where(forced, 1e30, jnp.where(future, -1.0, imp))
        cnt = jnp.zeros((nblk, NSA_QB), F32)
        for jp in range(nblk):
            row = v[jp:jp + 1, :]
            tie = jnp.where(jidx > jp, 1.0, 0.0)
            cnt = cnt + jnp.where(row > v, 1.0, jnp.where(row == v, tie, 0.0))
        sel = jnp.where(cnt < float(min(SEL_TOP, nblk)), 1.0, 0.0)
        sel_scr[8:8 + nblk, :] = sel
        far_scr[8:8 + nblk, :] = jnp.where(jidx <= 2 * i - 3, sel, 0.0)

        n0 = pl.multiple_of(NSA_QB * i + WINDOW - 128, 128)
        sn = _dot(ksel_ref[pl.ds(n0, 256), :], q_g) + tnear_ref[:, cs]
        sn = jnp.where(sel_rows(sel_scr, 2 * i - 2), sn, NEG)
        m_s = jnp.max(sn, axis=0, keepdims=True)
        e = jnp.exp(sn - m_s)
        l_s = jnp.sum(e, axis=0, keepdims=True)
        acc = _dot_tn(vsel_ref[pl.ds(n0, 256), :], e)

        def far_body(c, carry):
            m_o, l_o, a_o = carry
            k0 = pl.multiple_of(WINDOW + 256 * c, 256)
            sf = jnp.where(sel_rows(far_scr, 4 * c), _dot(ksel_ref[pl.ds(k0, 256), :], q_g), NEG)
            m_n = jnp.maximum(m_o, jnp.max(sf, axis=0, keepdims=True))
            alpha = jnp.exp(m_o - m_n)
            ef = jnp.exp(sf - m_n)
            l_n = alpha * l_o + jnp.sum(ef, axis=0, keepdims=True)
            a_n = alpha * a_o + _dot_tn(vsel_ref[pl.ds(k0, 256), :], ef)
            return m_n, l_n, a_n

        m_s, l_s, acc = lax.fori_loop(0, (2 * i + 1) // 4, far_body, (m_s, l_s, acc))
        o_s = acc / l_s

        w_start = pl.multiple_of(NSA_QB * i, 128)
        sw = _dot(kwin_ref[pl.ds(w_start, WINDOW + NSA_QB), :], q_g) + twin_ref[:, cs]
        kidx = lax.broadcasted_iota(jnp.int32, sw.shape, 0)
        sw = jnp.where(kidx >= WINDOW - NSA_QB * i, sw, NEG)
        m_w = jnp.max(sw, axis=0, keepdims=True)
        e = jnp.exp(sw - m_w)
        l_w = jnp.sum(e, axis=0, keepdims=True)
        o_w = _dot_tn(vwin_ref[pl.ds(w_start, WINDOW + NSA_QB), :], e) / l_w

        for h in range(4):
            hs = slice(128 * h, 128 * h + 128)
            ds_ = slice(64 * g, 64 * g + 64)
            gi = (4 * g + h) * 3
            out_rows.append(gt[gi:gi + 1, :] * o_c[ds_, hs] + gt[gi + 1:gi + 2, :] * o_s[ds_, hs]
                            + gt[gi + 2:gi + 3, :] * o_w[ds_, hs])
    out_ref[...] = jnp.concatenate(out_rows, axis=0).T.astype(out_ref.dtype)


def _nsa_attention(z, kcmp, vcmp, kv4, tables):
    B, S, _ = z.shape
    n_steps = S // NSA_QB
    nblk = S // SEL_BLOCK
    ncmp = kcmp.shape[1]
    t_cmp, t_near, t_win = tables
    cmpa = (CMP_STRIDE * np.arange(ncmp)[:, None] + (CMP_LEN - 1) - np.arange(NSA_QB)[None, :]).astype(np.int32)
    cmpa = jnp.asarray(np.tile(cmpa, (1, 4)))
    c = np.arange(ncmp)[None, :]
    j = np.arange(nblk)[:, None]
    ovl = ((CMP_STRIDE * c < (j + 1) * SEL_BLOCK) & (CMP_STRIDE * c + CMP_LEN > j * SEL_BLOCK)
           & (c < (S - CMP_LEN) // CMP_STRIDE + 1))
    ovl = jnp.asarray(ovl.astype(np.float32), dtype=_MM)
    sp = S + WINDOW
    const = lambda shape: pl.BlockSpec(shape, lambda b, i: (0,) * len(shape))
    return pl.pallas_call(
        _nsa_kernel,
        grid=(B, n_steps),
        in_specs=[
            pl.BlockSpec((None, NSA_QB, 512), lambda b, i: (b, i, Z_Q // 512)),
            pl.BlockSpec((None, NSA_QB, 128), lambda b, i: (b, i, Z_SMALL // 128)),
            pl.BlockSpec((None, ncmp, 128), lambda b, i: (b, 0, 0)),
            pl.BlockSpec((None, ncmp, 128), lambda b, i: (b, 0, 0)),
            pl.BlockSpec((None, sp, 128), lambda b, i: (b, 0, 0)),
            pl.BlockSpec((None, sp, 128), lambda b, i: (b, 0, 1)),
            pl.BlockSpec((None, sp, 128), lambda b, i: (b, 0, 2)),
            pl.BlockSpec((None, sp, 128), lambda b, i: (b, 0, 3)),
            const((ncmp, 512)), const((24, 1024)), const((256, 1024)), const((WINDOW + NSA_QB, 1024)),
            const((nblk, ncmp)),
        ],
        out_specs=pl.BlockSpec((None, NSA_QB, 512), lambda b, i: (b, i, 0)),
        out_shape=jax.ShapeDtypeStruct((B, S, 512), _ACT),
        scratch_shapes=[pltpu.VMEM((16 + ncmp, 512), F32), pltpu.VMEM((8 + nblk, NSA_QB), F32),
                        pltpu.VMEM((8 + nblk, NSA_QB), F32)],
        compiler_params=_cp("parallel", "arbitrary"),
        name="nsa_attention",
    )(z, z, kcmp, vcmp, kv4, kv4, kv4, kv4, cmpa, t_cmp, t_near, t_win, ovl)


CONV_T = 256
CONV_SUB = 64
CONV_HALO = 32


def _conv_kernel(z_ref, w_ref, b_ref, g_ref, beta_ref, o_ref, ubuf):
    t = pl.program_id(1)

    @pl.when(t == 0)
    def _():
        ubuf[0:CONV_HALO, :] = jnp.zeros((CONV_HALO, CONV_WIDTH), F32)

    zt = z_ref[...].astype(F32)
    ubuf[CONV_HALO:CONV_HALO + CONV_T, :] = zt[:, :CONV_WIDTH] * _sigmoid(zt[:, CONV_WIDTH:])
    for r in range(0, CONV_T, CONV_SUB):
        acc = jnp.zeros((CONV_SUB, CONV_WIDTH), F32)
        for k in range(CONV_K):
            off = CONV_HALO + r - (CONV_K - 1) + k
            acc = acc + w_ref[k:k + 1, :] * ubuf[off:off + CONV_SUB, :]
        y = _layer_norm(acc + b_ref[...], g_ref[...], beta_ref[...])
        o_ref[r:r + CONV_SUB, :] = (y * _sigmoid(y)).astype(o_ref.dtype)
    ubuf[0:CONV_HALO, :] = ubuf[CONV_T:CONV_T + CONV_HALO, :]


def _conformer_conv(z, conv_w, conv_b, conv_g, conv_beta):
    B, S, _ = z.shape
    w = jnp.concatenate([conv_w, jnp.zeros((1, CONV_WIDTH), conv_w.dtype)], axis=0).astype(F32)
    row = lambda a: a.reshape(1, CONV_WIDTH).astype(F32)
    const = lambda shape: pl.BlockSpec(shape, lambda b, t: (0, 0))
    return pl.pallas_call(
        _conv_kernel,
        grid=(B, S // CONV_T),
        in_specs=[pl.BlockSpec((None, CONV_T, 1024), lambda b, t: (b, t, Z_CONV // 1024)),
                  const((CONV_K + 1, CONV_WIDTH)), const((1, CONV_WIDTH)), const((1, CONV_WIDTH)),
                  const((1, CONV_WIDTH))],
        out_specs=pl.BlockSpec((None, CONV_T, CONV_WIDTH), lambda b, t: (b, t, 0)),
        out_shape=jax.ShapeDtypeStruct((B, S, CONV_WIDTH), _ACT),
        scratch_shapes=[pltpu.VMEM((CONV_HALO + CONV_T, CONV_WIDTH), F32)],
        compiler_params=_cp("parallel", "arbitrary"),
        name="conformer_conv",
    )(z, w, row(conv_b), row(conv_g), row(conv_beta))


GLA_T = 256
_GLA_LEVELS = (1, 2, 4, 8, 16, 32)


def _gla_masks():
    t = np.arange(GLA_CHUNK)[:, None]
    s = np.arange(GLA_CHUNK)[None, :]
    ms = [(t == s)]
    for c in _GLA_LEVELS:
        ms.append(((t // c) % 2 == 1) & (s // c == t // c - 1))
    m = np.stack(ms).astype(np.float32)
    return np.tile(m, (1, GLA_HEADS, 1))


def _gla_kernel(gq_ref, gk_ref, gv_ref, sm_ref, gr_ref, gw_ref, gb_ref, gg_ref, tril_ref, lm_ref, hm_ref,
                o_ref, state):
    @pl.when(pl.program_id(1) == 0)
    def _():
        state[...] = jnp.zeros(state.shape, F32)

    C = GLA_CHUNK
    tidx = lax.broadcasted_iota(jnp.int32, (C, GLA_HEADS * GLA_DK), 0)

    def stack_heads(x):
        return jnp.concatenate([x * hm_ref[h:h + 1, :] for h in range(GLA_HEADS)], axis=0)

    def chunk(ci, carry):
        r0 = pl.multiple_of(ci * C, C)
        q = gq_ref[pl.ds(r0, C), :].astype(F32) * (GLA_DK ** -0.5)
        k = gk_ref[pl.ds(r0, C), :].astype(F32)
        v = gv_ref[pl.ds(r0, C), :]
        pre = _dot(sm_ref[pl.ds(r0, C), :], gw_ref[...]) + gb_ref[...]
        la = (jnp.minimum(pre, 0.0) - jnp.log1p(jnp.exp(-jnp.abs(pre)))) * (1.0 / GLA_TAU)
        b = _dot01_left(tril_ref[...], la)
        attn = _dot_nt(stack_heads(q), k) * lm_ref[0]
        bstart = b
        bnext = pltpu.roll(b, C - 1, axis=0)
        for li, c in enumerate(_GLA_LEVELS):
            odd = (tidx // c) % 2 == 1
            q_l = jnp.where(odd, q * jnp.exp(b - bstart), 0.0)
            k_l = jnp.where(odd, 0.0, k * jnp.exp(bnext - b))
            attn = attn + _dot_nt(stack_heads(q_l), k_l) * lm_ref[li + 1]
            half = (tidx % (2 * c)) < c
            bstart = jnp.where(half, bstart, pltpu.roll(bstart, c, axis=0))
            bnext = jnp.where(half, pltpu.roll(bnext, C - c, axis=0), bnext)
        st = state[...]
        r_intra = _dot(attn, v)
        r_inter = _dot_nt(stack_heads(q * jnp.exp(b)), st)
        b_last = b[C - 1:C, :]
        upd = _dot_tn(v, k * jnp.exp(b_last - b))
        new_st = st * jnp.exp(b_last) + jnp.concatenate(
            [upd[128 * h:128 * h + 128, 64 * h:64 * h + 64] for h in range(GLA_HEADS)], axis=1)
        state[...] = new_st
        outs = []
        for h in range(GLA_HEADS):
            o = r_intra[64 * h:64 * h + 64, 128 * h:128 * h + 128] + r_inter[64 * h:64 * h + 64, :]
            ms = jnp.mean(o * o, axis=-1, keepdims=True)
            outs.append(o * lax.rsqrt(ms + 1e-6) * gg_ref[...])
        gr = gr_ref[pl.ds(r0, C), :].astype(F32)
        o_ref[pl.ds(r0, C), :] = (jnp.concatenate(outs, axis=1) * (gr * _sigmoid(gr))).astype(o_ref.dtype)
        return carry

    lax.fori_loop(0, GLA_T // C, chunk, 0)


def _gla(z, gla_w, gla_b, gla_g):
    B, S, _ = z.shape
    gw = jnp.zeros((128, GLA_HEADS * GLA_DK), F32).at[SMALL_GA:SMALL_GA + GLA_RANK].set(gla_w).astype(_MM)
    tril = jnp.asarray(np.tril(np.ones((GLA_CHUNK, GLA_CHUNK), np.float32)), dtype=_MM)
    lm = jnp.asarray(_gla_masks())
    hm = jnp.asarray(np.repeat(np.eye(GLA_HEADS, dtype=np.float32), GLA_DK, axis=1))
    const = lambda shape: pl.BlockSpec(shape, lambda b, t: (0,) * len(shape))
    zspec = lambda w, off: pl.BlockSpec((None, GLA_T, w), lambda b, t: (b, t, off // w))
    return pl.pallas_call(
        _gla_kernel,
        grid=(B, S // GLA_T),
        in_specs=[zspec(256, Z_GQ), zspec(256, Z_GK), zspec(512, Z_GV), zspec(128, Z_SMALL), zspec(512, Z_GR),
                  const((128, 256)), const((1, 256)), const((1, GLA_DV)), const((GLA_CHUNK, GLA_CHUNK)),
                  const((7, GLA_HEADS * GLA_CHUNK, GLA_CHUNK)), const((GLA_HEADS, 256))],
        out_specs=pl.BlockSpec((None, GLA_T, 512), lambda b, t: (b, t, 0)),
        out_shape=jax.ShapeDtypeStruct((B, S, 512), _ACT),
        scratch_shapes=[pltpu.VMEM((GLA_DV, GLA_HEADS * GLA_DK), F32)],
        compiler_params=_cp("parallel", "arbitrary"),
        name="gla",
    )(z, z, z, z, z, gw, gla_b.reshape(1, -1).astype(F32), gla_g.reshape(1, -1).astype(F32), tril, lm, hm)


MERGE_T = 512


def _merge_kernel(oa_ref, ob_ref, oc_ref, mg_ref, wb_ref, wo_ref, x_ref, g_ref, b_ref, o_ref):
    merged = None
    for j, r in enumerate((oa_ref, ob_ref, oc_ref)):
        gate = _sigmoid(mg_ref[:, D_MODEL * j:D_MODEL * (j + 1)].astype(F32))
        term = gate * _dot(r[...], wb_ref[j])
        merged = term if merged is None else merged + term
    mix = _dot(merged, wo_ref[...])
    o_ref[...] = _layer_norm(DEEPNORM_ALPHA * x_ref[...] + mix, g_ref[...], b_ref[...])


def _merge(oa, ob, oc, z2, w_branch, w_out, x2, g, b):
    n = x2.shape[0]
    tok = lambda w, blk=0: pl.BlockSpec((MERGE_T, w), lambda i: (i, blk))
    const = lambda shape: pl.BlockSpec(shape, lambda i: (0,) * len(shape))
    return pl.pallas_call(
        _merge_kernel,
        grid=(n // MERGE_T,),
        in_specs=[tok(512), tok(512), tok(512), tok(3072, Z_MERGE // 3072), const((3, 512, D_MODEL)),
                  const((D_MODEL, D_MODEL)), tok(D_MODEL), const((1, D_MODEL)), const((1, D_MODEL))],
        out_specs=tok(D_MODEL),
        out_shape=jax.ShapeDtypeStruct((n, D_MODEL), F32),
        compiler_params=_cp("parallel"),
        name="merge_outproj_ln",
    )(oa, ob, oc, z2, w_branch.astype(_MM), w_out.astype(_MM), x2, g.reshape(1, -1), b.reshape(1, -1))


XA_T = 512


def _xattn_kernel(x_ref, kv_ref, wq_ref, wo_ref, g_ref, b_ref, o_ref):
    x = x_ref[...]
    q = (_dot(x, wq_ref[...]) * (XA_DH ** -0.5)).astype(_MM)
    heads = []
    for h in range(XA_HEADS):
        hs = slice(XA_DH * h, XA_DH * (h + 1))
        s = _dot_nt(q[:, hs], kv_ref[:, hs])
        e = jnp.exp(s - jnp.max(s, axis=-1, keepdims=True))
        p = e / jnp.sum(e, axis=-1, keepdims=True)
        heads.append(_dot(p, kv_ref[:, D_MODEL + XA_DH * h:D_MODEL + XA_DH * (h + 1)]))
    xa = _dot(jnp.concatenate(heads, axis=1), wo_ref[...])
    o_ref[...] = _layer_norm(DEEPNORM_ALPHA * x + xa, g_ref[...], b_ref[...])


def _cross_attention(x, kv, wq, wo, g, b):
    B, S, _ = x.shape
    M = kv.shape[1]
    const = lambda shape: pl.BlockSpec(shape, lambda bb, t: (0,) * len(shape))
    return pl.pallas_call(
        _xattn_kernel,
        grid=(B, S // XA_T),
        in_specs=[pl.BlockSpec((None, XA_T, D_MODEL), lambda bb, t: (bb, t, 0)),
                  pl.BlockSpec((None, M, 2 * D_MODEL), lambda bb, t: (bb, 0, 0)),
                  const((D_MODEL, D_MODEL)), const((D_MODEL, D_MODEL)), const((1, D_MODEL)), const((1, D_MODEL))],
        out_specs=pl.BlockSpec((None, XA_T, D_MODEL), lambda bb, t: (bb, t, 0)),
        out_shape=jax.ShapeDtypeStruct((B, S, D_MODEL), F32),
        compiler_params=_cp("parallel", "parallel"),
        name="cross_attention_ln",
    )(x, kv, wq.astype(_MM), wo.astype(_MM), g.reshape(1, -1), b.reshape(1, -1))


ROUTE_T = 512


def _router_kernel(x_ref, wt_ref, b_ref, upper_ref, ti_ref, gate_ref, rank_ref, cnt_ref, carry):
    @pl.when(pl.program_id(0) == 0)
    def _():
        carry[...] = jnp.zeros(carry.shape, F32)

    x = x_ref[...]
    xh = x.astype(_MM)
    xl = (x - xh.astype(F32)).astype(_MM)
    w = wt_ref[...]
    wh = w.astype(_MM)
    wl = (w - wh.astype(F32)).astype(_MM)
    logits = _dot_nt(wh, xh) + _dot_nt(wh, xl) + _dot_nt(wl, xh) + b_ref[...]
    eidx = lax.broadcasted_iota(jnp.int32, logits.shape, 0)
    v = logits
    tops, idxs, hots = [], [], []
    for _ in range(TOP_K):
        m = jnp.max(v, axis=0, keepdims=True)
        idx = jnp.min(jnp.where(v == m, eidx, N_EXPERTS), axis=0, keepdims=True)
        hot = eidx == idx
        v = jnp.where(hot, -jnp.inf, v)
        tops.append(m)
        idxs.append(idx)
        hots.append(jnp.where(hot, 1.0, 0.0))
    es = [jnp.exp(t - tops[0]) for t in tops]
    den = es[0] + es[1] + es[2] + es[3]
    multihot = hots[0] + hots[1] + hots[2] + hots[3]
    before = jnp.dot(multihot.astype(_MM), upper_ref[...], preferred_element_type=F32) + carry[...][:, 0:1]
    ranks = [jnp.sum(h * before, axis=0, keepdims=True) for h in hots]
    pad = jnp.zeros((8 - TOP_K, x.shape[0]), F32)
    ti_ref[...] = jnp.concatenate(idxs + [pad.astype(jnp.int32)], axis=0)
    gate_ref[...] = jnp.concatenate([e / den for e in es] + [pad], axis=0)
    rank_ref[...] = jnp.concatenate(ranks + [pad], axis=0).astype(jnp.int32)
    carry[...] = carry[...] + jnp.sum(multihot, axis=1, keepdims=True)
    cnt_ref[...] = carry[...]


def _router(x2, router_w, router_b):
    n = x2.shape[0]
    upper = jnp.asarray(np.triu(np.ones((ROUTE_T, ROUTE_T), np.float32), 1), dtype=_MM)
    bcol = jnp.broadcast_to(router_b.astype(F32)[:, None], (N_EXPERTS, ROUTE_T))
    tokspec = pl.BlockSpec((8, ROUTE_T), lambda i: (0, i))
    const = lambda shape: pl.BlockSpec(shape, lambda i: (0,) * len(shape))
    return pl.pallas_call(
        _router_kernel,
        grid=(n // ROUTE_T,),
        in_specs=[pl.BlockSpec((ROUTE_T, D_MODEL), lambda i: (i, 0)), const((N_EXPERTS, D_MODEL)),
                  const((N_EXPERTS, ROUTE_T)), const((ROUTE_T, ROUTE_T))],
        out_specs=[tokspec, tokspec, tokspec, const((N_EXPERTS, 128))],
        out_shape=[jax.ShapeDtypeStruct((8, n), jnp.int32), jax.ShapeDtypeStruct((8, n), F32),
                   jax.ShapeDtypeStruct((8, n), jnp.int32), jax.ShapeDtypeStruct((N_EXPERTS, 128), F32)],
        scratch_shapes=[pltpu.VMEM((N_EXPERTS, 128), F32)],
        compiler_params=_cp("arbitrary"),
        name="moe_router",
    )(x2, router_w.T.astype(F32), bcol, upper)


def _moe_kernel(blk_e_ref, nvalid_ref, tokc_ref, tokn_ref, x_hbm, wgu_ref, bgu_ref, wdn_ref, bdn_ref, y_ref,
                xbuf, sem):
    b = pl.program_id(0)
    nvalid = nvalid_ref[0]
    slot = b % 2

    def gather(tok_ref, s):
        def body(r, c):
            pltpu.make_async_copy(x_hbm.at[pl.ds(tok_ref[0, r], 1), :], xbuf.at[s, pl.ds(r, 1), :],
                                  sem.at[s]).start()
            return c
        lax.fori_loop(0, MOE_BLK, body, 0)

    @pl.when(jnp.logical_and(b == 0, nvalid > 0))
    def _():
        gather(tokc_ref, 0)

    @pl.when(b + 1 < nvalid)
    def _():
        gather(tokn_ref, 1 - slot)

    @pl.when(b < nvalid)
    def _():
        pltpu.make_async_copy(x_hbm.at[pl.ds(0, MOE_BLK), :], xbuf.at[slot], sem.at[slot]).wait()
        gu = _dot(xbuf[slot], wgu_ref[...]) + bgu_ref[...]
        g = jnp.minimum(gu[:, :D_MODEL], SWIGLU_LIMIT)
        u = jnp.clip(gu[:, D_MODEL:], -SWIGLU_LIMIT, SWIGLU_LIMIT)
        h = (u + 1.0) * g * _sigmoid(SWIGLU_ALPHA * g)
        y_ref[...] = _dot(h, wdn_ref[...]) + bdn_ref[...]

    @pl.when(b >= nvalid)
    def _():
        y_ref[...] = jnp.zeros(y_ref.shape, F32)


def _moe_experts(x2, slot_tok, blk_e, nvalid, w_gu, b_gu, w_dn, b_dn):
    n_blocks = blk_e.shape[0]
    tok3 = slot_tok.reshape(n_blocks, 1, MOE_BLK)
    smem = lambda f: pl.BlockSpec((None, 1, MOE_BLK), f, memory_space=pltpu.SMEM)
    grid_spec = pltpu.PrefetchScalarGridSpec(
        num_scalar_prefetch=2,
        grid=(n_blocks,),
        in_specs=[
            smem(lambda b, be, nv: (b, 0, 0)),
            smem(lambda b, be, nv: (jnp.minimum(b + 1, n_blocks - 1), 0, 0)),
            pl.BlockSpec(memory_space=pl.ANY),
            pl.BlockSpec((None, D_MODEL, 2 * D_MODEL), lambda b, be, nv: (be[b], 0, 0)),
            pl.BlockSpec((None, 1, 2 * D_MODEL), lambda b, be, nv: (be[b], 0, 0)),
            pl.BlockSpec((None, D_MODEL, D_MODEL), lambda b, be, nv: (be[b], 0, 0)),
            pl.BlockSpec((None, 1, D_MODEL), lambda b, be, nv: (be[b], 0, 0)),
        ],
        out_specs=pl.BlockSpec((MOE_BLK, D_MODEL), lambda b, be, nv: (b, 0)),
        scratch_shapes=[pltpu.VMEM((2, MOE_BLK, D_MODEL), F32), pltpu.SemaphoreType.DMA((2,))],
    )
    return pl.pallas_call(
        _moe_kernel,
        grid_spec=grid_spec,
        out_shape=jax.ShapeDtypeStruct((n_blocks * MOE_BLK, D_MODEL), F32),
        compiler_params=_cp("arbitrary"),
        name="moe_experts",
    )(blk_e, nvalid, tok3, tok3, x2, w_gu.astype(_MM), b_gu.reshape(N_EXPERTS, 1, -1), w_dn.astype(_MM),
      b_dn.reshape(N_EXPERTS, 1, -1))


def _combine_kernel(dc_ref, dn_ref, y_hbm, gate_ref, x_ref, g_ref, b_ref, o_ref, buf, sem):
    i = pl.program_id(0)
    n = pl.num_programs(0)
    slot = i % 2
    rows = TOP_K * COMB_T

    def gather(d_ref, s):
        def body(r, c):
            pltpu.make_async_copy(y_hbm.at[pl.ds(d_ref[0, r], 1), :], buf.at[s, pl.ds(r, 1), :], sem.at[s]).start()
            return c
        lax.fori_loop(0, rows, body, 0)

    @pl.when(i == 0)
    def _():
        gather(dc_ref, 0)

    @pl.when(i + 1 < n)
    def _():
        gather(dn_ref, 1 - slot)

    pltpu.make_async_copy(y_hbm.at[pl.ds(0, rows), :], buf.at[slot], sem.at[slot]).wait()
    gate = gate_ref[...]
    ff = gate[:, 0:1] * buf[slot, 0:COMB_T, :]
    for k in range(1, TOP_K):
        ff = ff + gate[:, k:k + 1] * buf[slot, k * COMB_T:(k + 1) * COMB_T, :]
    o_ref[...] = _layer_norm(DEEPNORM_ALPHA * x_ref[...] + ff, g_ref[...], b_ref[...])


def _moe_combine(ys, dest, gate_t, x2, g, b):
    n = x2.shape[0]
    steps = n // COMB_T
    d3 = dest.reshape(TOP_K, steps, COMB_T).transpose(1, 0, 2).reshape(steps, 1, TOP_K * COMB_T)
    smem = lambda f: pl.BlockSpec((None, 1, TOP_K * COMB_T), f, memory_space=pltpu.SMEM)
    const = lambda shape: pl.BlockSpec(shape, lambda i: (0,) * len(shape))
    return pl.pallas_call(
        _combine_kernel,
        grid=(steps,),
        in_specs=[smem(lambda i: (i, 0, 0)), smem(lambda i: (jnp.minimum(i + 1, steps - 1), 0, 0)),
                  pl.BlockSpec(memory_space=pl.ANY), pl.BlockSpec((COMB_T, 8), lambda i: (i, 0)),
                  pl.BlockSpec((COMB_T, D_MODEL), lambda i: (i, 0)), const((1, D_MODEL)), const((1, D_MODEL))],
        out_specs=pl.BlockSpec((COMB_T, D_MODEL), lambda i: (i, 0)),
        out_shape=jax.ShapeDtypeStruct((n, D_MODEL), F32),
        scratch_shapes=[pltpu.VMEM((2, TOP_K * COMB_T, D_MODEL), F32), pltpu.SemaphoreType.DMA((2,))],
        compiler_params=_cp("arbitrary"),
        name="moe_combine_ln",
    )(d3, d3, ys, gate_t, x2, g.reshape(1, -1), b.reshape(1, -1))


def _moe_ffn(x2, router_w, router_b, w_gu, b_gu, w_dn, b_dn, g, b):
    n = x2.shape[0]
    a = n * TOP_K
    top_i, gate, rank, cnt = _router(x2, router_w, router_b)
    counts = cnt[:, 0].astype(jnp.int32)
    padded = (counts + MOE_BLK - 1) // MOE_BLK * MOE_BLK
    pends = jnp.cumsum(padded)
    pstarts = pends - padded
    n_blocks = a // MOE_BLK + N_EXPERTS
    dest = pstarts[top_i[:TOP_K]] + rank[:TOP_K]
    tok = jnp.broadcast_to(jnp.arange(n, dtype=jnp.int32)[None, :], (TOP_K, n))
    slot_tok = jnp.zeros((n_blocks * MOE_BLK,), jnp.int32).at[dest.reshape(-1)].set(tok.reshape(-1))
    blk_e = jnp.minimum(jnp.searchsorted(pends, jnp.arange(n_blocks, dtype=jnp.int32) * MOE_BLK, side='right'),
                        N_EXPERTS - 1).astype(jnp.int32)
    nvalid = (pends[-1:] // MOE_BLK).astype(jnp.int32)
    ys = _moe_experts(x2, slot_tok, blk_e, nvalid, w_gu, b_gu, w_dn, b_dn)
    return _moe_combine(ys, dest, gate.T, x2, g, b)


def _layer(x, mem, tables, w_in, cmp_pe, cmp_w1, cmp_b1, cmp_w2, conv_w, conv_b, conv_g, conv_beta, gla_w, gla_b,
           gla_g, w_branch, w_out, xa_wq, xa_wkv, xa_wo, router_w, router_b, w_gu, b_gu, w_dn, b_dn, ng, nb):
    B, S, D = x.shape
    n = B * S
    x2 = x.reshape(n, D)
    z2 = _matmul(x2, _reorder_w_in(w_in), 1024, ZW // 5, _ACT)
    z = z2.reshape(B, S, ZW)
    kvc = z[:, :, Z_KV:Z_KV + 256].reshape(B, S // 16, 16, 2, 128).transpose(3, 0, 1, 2, 4)
    cmp = _compress(kvc.reshape(2, B, S // 16, 2048), cmp_pe, cmp_w1, cmp_b1, cmp_w2)
    kv4 = jnp.pad(z[:, :, Z_KV + 256:Z_KV + 768], ((0, 0), (WINDOW, 0), (0, 0)))
    oa = _nsa_attention(z, cmp[0], cmp[1], kv4, tables)
    ob = _conformer_conv(z, conv_w, conv_b, conv_g, conv_beta)
    oc = _gla(z, gla_w, gla_b, gla_g)
    x2 = _merge(oa.reshape(n, 512), ob.reshape(n, 512), oc.reshape(n, 512), z2, w_branch, w_out, x2, ng[0], nb[0])
    kv = _matmul(mem.reshape(-1, D), xa_wkv.astype(_MM), 1024, 1024, _ACT).reshape(B, -1, 2 * D)
    x2 = _cross_attention(x2.reshape(B, S, D), kv, xa_wq, xa_wo, ng[1], nb[1]).reshape(n, D)
    x2 = _moe_ffn(x2, router_w, router_b, w_gu, b_gu, w_dn, b_dn, ng[2], nb[2])
    return x2.reshape(B, S, D)


def kernel(x, mem, rel_bias, w_in, cmp_pe, cmp_w1, cmp_b1, cmp_w2, conv_w, conv_b, conv_norm_g, conv_norm_b,
           gla_gate_w, gla_gate_b, gla_norm_g, w_branch, w_out, xa_wq, xa_wkv, xa_wo, router_w, router_b,
           expert_w_gu, expert_b_gu, expert_w_down, expert_b_down, norm_g, norm_b):
    tables = _nsa_tables(rel_bias)
    for l in range(DEPTH):
        x = _layer(x, mem, tables, w_in[l], cmp_pe[l], cmp_w1[l], cmp_b1[l], cmp_w2[l], conv_w[l], conv_b[l],
                   conv_norm_g[l], conv_norm_b[l], gla_gate_w[l], gla_gate_b[l], gla_norm_g[l], w_branch[l],
                   w_out[l], xa_wq[l], xa_wkv[l], xa_wo[l], router_w[l], router_b[l], expert_w_gu[l],
                   expert_b_gu[l], expert_w_down[l], expert_b_down[l], norm_g[l], norm_b[l])
    return x
```

```python
import functools
import math

import numpy as np
import jax
import jax.numpy as jnp
from jax import lax
from jax.experimental import pallas as pl
from jax.experimental.pallas import tpu as pltpu

F32 = jnp.float32
_MM = jnp.bfloat16
_ACT = jnp.bfloat16
NEG = -1e30
LOG2E = 1.4426950408889634

D_MODEL = 1024
DEPTH = 2
MEM_LEN = 256
NSA_HEADS = 8
NSA_DH = 64
CMP_LEN = 32
CMP_STRIDE = 16
SEL_BLOCK = 64
SEL_TOP = 16
WINDOW = 512
CONV_WIDTH = 512
CONV_K = 31
GLA_HEADS = 4
GLA_DK = 64
GLA_DV = 128
GLA_RANK = 16
GLA_TAU = 16.0
GLA_CHUNK = 64
REL_BUCKETS = 32
REL_MAX_DIST = 128
XA_HEADS = 4
XA_DH = 256
N_EXPERTS = 32
TOP_K = 4
SWIGLU_ALPHA = 1.702
SWIGLU_LIMIT = 7.0
DEEPNORM_ALPHA = (2 * DEPTH) ** 0.25

Z_MERGE = 0
Z_CONV = 3072
Z_Q = 4096
Z_GV = 4608
Z_GR = 5120
Z_KV = 5632
Z_GQ = 6400
Z_GK = 6656
Z_SMALL = 6912
ZW = 7040
SMALL_GA = 24

_IN_SIZES = (512, 768, 24, 1024, 256, 256, 512, 16, 512, 3072)
_IN_OFF = np.concatenate([[0], np.cumsum(_IN_SIZES)]).astype(int)

VMEM_LIMIT = 56 * 1024 * 1024

NSA_QB = 128
MOE_BLK = 512
COMB_T = 256


def _cp(*sem):
    return pltpu.CompilerParams(dimension_semantics=sem, vmem_limit_bytes=VMEM_LIMIT)


def _dot(a, b):
    return jnp.dot(a.astype(_MM), b.astype(_MM), preferred_element_type=F32)


def _dot_nt(a, b):
    return lax.dot_general(a.astype(_MM), b.astype(_MM), (((1,), (1,)), ((), ())), preferred_element_type=F32)


def _dot_tn(a, b):
    return lax.dot_general(a.astype(_MM), b.astype(_MM), (((0,), (0,)), ((), ())), preferred_element_type=F32)


def _split3(x):
    x1 = x.astype(_MM)
    r1 = x - x1.astype(F32)
    x2 = r1.astype(_MM)
    x3 = (r1 - x2.astype(F32)).astype(_MM)
    return x1, x2, x3


def _dot01_left(m01, x):
    x1, x2, x3 = _split3(x)
    return (jnp.dot(m01, x1, preferred_element_type=F32) + jnp.dot(m01, x2, preferred_element_type=F32)
            + jnp.dot(m01, x3, preferred_element_type=F32))


def _layer_norm(y, g, b, eps=1e-5):
    mu = jnp.mean(y, axis=-1, keepdims=True)
    d = y - mu
    var = jnp.mean(d * d, axis=-1, keepdims=True)
    return d * lax.rsqrt(var + eps) * g + b


def _sigmoid(x):
    return 1.0 / (1.0 + jnp.exp(-x))


def _mm_kernel(x_ref, w_ref, o_ref):
    o_ref[...] = _dot(x_ref[...], w_ref[...]).astype(o_ref.dtype)


def _matmul(x, w, tm, tn, out_dtype):
    m, k = x.shape
    n = w.shape[1]
    tm = min(tm, m)
    return pl.pallas_call(
        _mm_kernel,
        grid=(m // tm, n // tn),
        in_specs=[pl.BlockSpec((tm, k), lambda i, j: (i, 0)), pl.BlockSpec((k, tn), lambda i, j: (0, j))],
        out_specs=pl.BlockSpec((tm, tn), lambda i, j: (i, j)),
        out_shape=jax.ShapeDtypeStruct((m, n), out_dtype),
        compiler_params=_cp("parallel", "parallel"),
        name="matmul",
    )(x, w)


def _reorder_w_in(w_in):
    def cols(i):
        return w_in[:, _IN_OFF[i]:_IN_OFF[i + 1]]
    small = jnp.concatenate([cols(2), cols(7), jnp.zeros((D_MODEL, 128 - 40), w_in.dtype)], axis=1)
    w = jnp.concatenate([cols(9), cols(3), cols(0), cols(6), cols(8), cols(1), cols(4), cols(5), small], axis=1)
    return w.astype(_MM)


def _cmp_kernel(x_ref, pea_ref, peb_ref, wa_ref, wb_ref, b1_ref, w2_ref, o_ref):
    x = x_ref[...].astype(F32)
    a = _dot(x + pea_ref[...], wa_ref[...])
    bm = _dot(x + peb_ref[...], wb_ref[...])
    n = bm.shape[0]
    bs = pltpu.roll(bm, n - 1, axis=0)
    h = jax.nn.gelu(a + bs + b1_ref[...], approximate=True)
    o_ref[...] = _dot(h, w2_ref[...]).astype(o_ref.dtype)


def _compress(kv2, cmp_pe, cmp_w1, cmp_b1, cmp_w2):
    _, B, R, _ = kv2.shape
    eye2 = jnp.eye(2, dtype=F32)

    def half(w1h):
        w = jnp.einsum('rlde,gh->rlgdhe', w1h, eye2)
        return w.reshape(2, 16 * 128, 128).astype(_MM)

    wa = half(cmp_w1[:, :16])
    wb = half(cmp_w1[:, 16:])

    def pe_half(p):
        return jnp.broadcast_to(p[:, :, None, :], (2, 16, 2, 64)).reshape(2, 1, 2048).astype(F32)

    pea = pe_half(cmp_pe[:, :16])
    peb = pe_half(cmp_pe[:, 16:])
    b1 = jnp.tile(cmp_b1, (1, 2)).reshape(2, 1, 128).astype(F32)
    w2 = jnp.einsum('rde,gh->rgdhe', cmp_w2, eye2).reshape(2, 128, 128).astype(_MM)
    return pl.pallas_call(
        _cmp_kernel,
        grid=(2, B),
        in_specs=[
            pl.BlockSpec((None, None, R, 2048), lambda r, b: (r, b, 0, 0)),
            pl.BlockSpec((None, 1, 2048), lambda r, b: (r, 0, 0)),
            pl.BlockSpec((None, 1, 2048), lambda r, b: (r, 0, 0)),
            pl.BlockSpec((None, 2048, 128), lambda r, b: (r, 0, 0)),
            pl.BlockSpec((None, 2048, 128), lambda r, b: (r, 0, 0)),
            pl.BlockSpec((None, 1, 128), lambda r, b: (r, 0, 0)),
            pl.BlockSpec((None, 128, 128), lambda r, b: (r, 0, 0)),
        ],
        out_specs=pl.BlockSpec((None, None, R, 128), lambda r, b: (r, b, 0, 0)),
        out_shape=jax.ShapeDtypeStruct((2, B, R, 128), _ACT),
        compiler_params=_cp("parallel", "parallel"),
        name="nsa_compress",
    )(kv2, pea, peb, wa, wb, b1, w2)


def _t5_bucket_np(dist):
    n = np.maximum(dist, 0)
    exact = REL_BUCKETS // 2
    lr = np.log(np.maximum(n, 1).astype(np.float32) / np.float32(exact)) / np.float32(math.log(REL_MAX_DIST / exact))
    large = exact + (lr * np.float32(REL_BUCKETS - exact)).astype(np.int32)
    return np.where(n < exact, n, np.minimum(large, REL_BUCKETS - 1))


def _nsa_tables(rel_bias):
    rel = (rel_bias - rel_bias[REL_BUCKETS - 1:REL_BUCKETS]).astype(F32) * LOG2E
    q = np.arange(NSA_QB)

    def table(dist, valid, fill):
        hot = jnp.asarray(_t5_bucket_np(dist), jnp.int32)[:, :, None] == jnp.arange(REL_BUCKETS)[None, None, :]
        t = jnp.sum(jnp.where(hot[..., None], rel[None, None], 0.0), axis=2)
        t = jnp.where(valid[:, :, None], t, fill)
        return jnp.transpose(t, (0, 2, 1)).reshape(dist.shape[0], NSA_HEADS * NSA_QB)

    cc = np.arange(24) - 16
    d_c = q[None, :] - CMP_STRIDE * cc[:, None] - (CMP_LEN - 1)
    t_cmp = table(d_c, d_c >= 0, 0.0)
    ko = np.arange(256)
    d_n = q[None, :] + 128 - ko[:, None]
    t_near = table(d_n, d_n >= 0, NEG)
    kw = np.arange(WINDOW + NSA_QB)
    d_w = q[None, :] + WINDOW - kw[:, None]
    t_win = table(d_w, (d_w >= 0) & (d_w < WINDOW), NEG)
    return t_cmp, t_near, t_win


def _nsa_kernel(q_ref, gate_ref, kcmp_ref, vcmp_ref, ksel_ref, vsel_ref, kwin_ref, vwin_ref,
                cmpa_ref, tcmp_ref, tnear_ref, twin_ref, ovl_ref, out_ref, s_scr, v_scr, sel_scr, far_scr):
    i = pl.program_id(1)
    ncmp = kcmp_ref.shape[0]
    nblk = ovl_ref.shape[0]
    qt = (q_ref[...].astype(F32) * (NSA_DH ** -0.5 * LOG2E)).T
    gt = _sigmoid(gate_ref[...].astype(F32)).T
    zeros64 = jnp.zeros((64, NSA_QB), F32)
    jidx = lax.broadcasted_iota(jnp.int32, (nblk, NSA_QB), 0)
    qidx = lax.broadcasted_iota(jnp.int32, (nblk, NSA_QB), 1)
    sub8 = lax.broadcasted_iota(jnp.int32, (8, NSA_QB), 0)
    cur = 2 * i + (qidx >= SEL_BLOCK).astype(jnp.int32)
    forced = (jidx == 0) | (jidx == cur) | (jidx == cur - 1)
    future = jidx > cur
    neg8 = jnp.full((8, NSA_QB), NEG, F32)
    s_scr[0:16, :] = jnp.zeros((16, 4 * NSA_QB), F32)

    def mask_rows(scr, first_block, n):
        rows = [jnp.broadcast_to(scr[pl.ds(8 + first_block + b, 1), :], (SEL_BLOCK, NSA_QB)) for b in range(n)]
        return jnp.concatenate(rows, axis=0)

    def add_mask(s, m):
        return jnp.concatenate([s[:, 128 * h:128 * h + 128] + m for h in range(4)], axis=1)

    q_gs, o_cs, o_ws, state = [], [], [], []
    for g in range(2):
        blocks = []
        for h in range(4):
            r = qt[64 * (4 * g + h):64 * (4 * g + h) + 64]
            blocks.append(jnp.concatenate([r, zeros64] if g == 0 else [zeros64, r], axis=0))
        q_g = jnp.concatenate(blocks, axis=1).astype(_MM)
        q_gs.append(q_g)
        cs = slice(512 * g, 512 * g + 512)

        s_scr[16:16 + ncmp, :] = _dot(kcmp_ref[...], q_g)
        w0 = pl.multiple_of(8 * i, 8)
        s_scr[pl.ds(w0, 24), :] = s_scr[pl.ds(w0, 24), :] + tcmp_ref[:, cs]
        sc = jnp.where(cmpa_ref[...] <= NSA_QB * i, s_scr[16:16 + ncmp, :], NEG)
        m = jnp.maximum(jnp.max(sc, axis=0, keepdims=True), 0.1 * NEG)
        e = jnp.exp2(sc - m)
        l = jnp.sum(e, axis=0, keepdims=True)
        pc = e / jnp.maximum(l, 1e-30)
        o_cs.append(_dot_tn(vcmp_ref[...], pc))

        psum = pc[:, 0:128] + pc[:, 128:256] + pc[:, 256:384] + pc[:, 384:512]
        imp = _dot01_left(ovl_ref[...], psum)
        v = jnp.where(forced, 1e30, jnp.where(future, -1.0, imp))
        v_scr[g] = v
        sel_scr[g, 0:8, :] = neg8
        far_scr[g, 0:8, :] = neg8
        for r8 in range(0, nblk, 8):
            vr = v[r8:r8 + 8]
            cnt = jnp.zeros((8, NSA_QB), F32)
            for jp in range(nblk):
                row = jnp.broadcast_to(v_scr[g, pl.ds(jp, 1), :], (8, NSA_QB))
                if jp < r8:
                    cnt = cnt + jnp.where(row >= vr, 1.0, 0.0)
                elif jp >= r8 + 8:
                    cnt = cnt + jnp.where(row > vr, 1.0, 0.0)
                else:
                    cnt = cnt + jnp.where(sub8 > jp - r8, jnp.where(row >= vr, 1.0, 0.0),
                                          jnp.where(row > vr, 1.0, 0.0))
            sel = jnp.where(cnt < float(min(SEL_TOP, nblk)), 0.0, NEG)
            sel_scr[g, 8 + r8:16 + r8, :] = sel
            far_scr[g, 8 + r8:16 + r8, :] = jnp.where(jidx[r8:r8 + 8] <= 2 * i - 3, sel, NEG)

        n0 = pl.multiple_of(NSA_QB * i + WINDOW - 128, 128)
        sn = add_mask(_dot(ksel_ref[pl.ds(n0, 256), :], q_g) + tnear_ref[:, cs], mask_rows(sel_scr.at[g], 2 * i - 2, 4))
        m_s = jnp.max(sn, axis=0, keepdims=True)
        e = jnp.exp2(sn - m_s)
        state += [m_s, jnp.sum(e, axis=0, keepdims=True), _dot_tn(vsel_ref[pl.ds(n0, 256), :], e)]

        w_start = pl.multiple_of(NSA_QB * i, 128)
        sw = _dot(kwin_ref[pl.ds(w_start, WINDOW + NSA_QB), :], q_g) + twin_ref[:, cs]
        kidx = lax.broadcasted_iota(jnp.int32, sw.shape, 0)
        sw = jnp.where(kidx >= WINDOW - NSA_QB * i, sw, NEG)
        m_w = jnp.max(sw, axis=0, keepdims=True)
        e = jnp.exp2(sw - m_w)
        l_w = jnp.sum(e, axis=0, keepdims=True)
        o_ws.append(_dot_tn(vwin_ref[pl.ds(w_start, WINDOW + NSA_QB), :], e) / l_w)

    def far_body(c, carry):
        k0 = pl.multiple_of(WINDOW + 512 * c, 512)
        kf = ksel_ref[pl.ds(k0, 512), :]
        vf = vsel_ref[pl.ds(k0, 512), :]
        new = []
        for g in range(2):
            m_o, l_o, a_o = carry[3 * g:3 * g + 3]
            sf = add_mask(_dot(kf, q_gs[g]), mask_rows(far_scr.at[g], 8 * c, 8))
            m_n = jnp.maximum(m_o, jnp.max(sf, axis=0, keepdims=True))
            alpha = jnp.exp2(m_o - m_n)
            ef = jnp.exp2(sf - m_n)
            new += [m_n, alpha * l_o + jnp.sum(ef, axis=0, keepdims=True), alpha * a_o + _dot_tn(vf, ef)]
        return tuple(new)

    state = lax.fori_loop(0, (2 * i + 5) // 8, far_body, tuple(state))

    out_rows = []
    for g in range(2):
        o_s = state[3 * g + 2] / state[3 * g + 1]
        for h in range(4):
            hs = slice(128 * h, 128 * h + 128)
            ds_ = slice(64 * g, 64 * g + 64)
            gi = (4 * g + h) * 3
            out_rows.append(gt[gi:gi + 1, :] * o_cs[g][ds_, hs] + gt[gi + 1:gi + 2, :] * o_s[ds_, hs]
                            + gt[gi + 2:gi + 3, :] * o_ws[g][ds_, hs])
    out_ref[...] = jnp.concatenate(out_rows, axis=0).T.astype(out_ref.dtype)


def _nsa_attention(z, kcmp, vcmp, kv4, tables):
    B, S, _ = z.shape
    n_steps = S // NSA_QB
    nblk = S // SEL_BLOCK
    ncmp = kcmp.shape[1]
    t_cmp, t_near, t_win = tables
    cmpa = (CMP_STRIDE * np.arange(ncmp)[:, None] + (CMP_LEN - 1) - np.arange(NSA_QB)[None, :]).astype(np.int32)
    cmpa = jnp.asarray(np.tile(cmpa, (1, 4)))
    c = np.arange(ncmp)[None, :]
    j = np.arange(nblk)[:, None]
    ovl = ((CMP_STRIDE * c < (j + 1) * SEL_BLOCK) & (CMP_STRIDE * c + CMP_LEN > j * SEL_BLOCK)
           & (c < (S - CMP_LEN) // CMP_STRIDE + 1))
    ovl = jnp.asarray(ovl.astype(np.float32), dtype=_MM)
    sp = S + WINDOW
    const = lambda shape: pl.BlockSpec(shape, lambda b, i: (0,) * len(shape))
    return pl.pallas_call(
        _nsa_kernel,
        grid=(B, n_steps),
        in_specs=[
            pl.BlockSpec((None, NSA_QB, 512), lambda b, i: (b, i, Z_Q // 512)),
            pl.BlockSpec((None, NSA_QB, 128), lambda b, i: (b, i, Z_SMALL // 128)),
            pl.BlockSpec((None, ncmp, 128), lambda b, i: (b, 0, 0)),
            pl.BlockSpec((None, ncmp, 128), lambda b, i: (b, 0, 0)),
            pl.BlockSpec((None, sp, 128), lambda b, i: (b, 0, 0)),
            pl.BlockSpec((None, sp, 128), lambda b, i: (b, 0, 1)),
            pl.BlockSpec((None, sp, 128), lambda b, i: (b, 0, 2)),
            pl.BlockSpec((None, sp, 128), lambda b, i: (b, 0, 3)),
            const((ncmp, 512)), const((24, 1024)), const((256, 1024)), const((WINDOW + NSA_QB, 1024)),
            const((nblk, ncmp)),
        ],
        out_specs=pl.BlockSpec((None, NSA_QB, 512), lambda b, i: (b, i, 0)),
        out_shape=jax.ShapeDtypeStruct((B, S, 512), _ACT),
        scratch_shapes=[pltpu.VMEM((16 + ncmp, 512), F32), pltpu.VMEM((2, nblk, NSA_QB), F32),
                        pltpu.VMEM((2, 8 + nblk, NSA_QB), F32), pltpu.VMEM((2, 8 + nblk, NSA_QB), F32)],
        compiler_params=_cp("parallel", "arbitrary"),
        name="nsa_attention",
    )(z, z, kcmp, vcmp, kv4, kv4, kv4, kv4, cmpa, t_cmp, t_near, t_win, ovl)


CONV_T = 256
CONV_SUB = 64
CONV_HALO = 32


def _conv_kernel(z_ref, w_ref, b_ref, g_ref, beta_ref, o_ref, ubuf):
    t = pl.program_id(1)

    @pl.when(t == 0)
    def _():
        ubuf[0:CONV_HALO, :] = jnp.zeros((CONV_HALO, CONV_WIDTH), F32)

    zt = z_ref[...].astype(F32)
    ubuf[CONV_HALO:CONV_HALO + CONV_T, :] = zt[:, :CONV_WIDTH] * _sigmoid(zt[:, CONV_WIDTH:])
    for r in range(0, CONV_T, CONV_SUB):
        acc = jnp.zeros((CONV_SUB, CONV_WIDTH), F32)
        for k in range(CONV_K):
            off = CONV_HALO + r - (CONV_K - 1) + k
            acc = acc + w_ref[k:k + 1, :] * ubuf[off:off + CONV_SUB, :]
        y = _layer_norm(acc + b_ref[...], g_ref[...], beta_ref[...])
        o_ref[r:r + CONV_SUB, :] = (y * _sigmoid(y)).astype(o_ref.dtype)
    ubuf[0:CONV_HALO, :] = ubuf[CONV_T:CONV_T + CONV_HALO, :]


def _conformer_conv(z, conv_w, conv_b, conv_g, conv_beta):
    B, S, _ = z.shape
    w = jnp.concatenate([conv_w, jnp.zeros((1, CONV_WIDTH), conv_w.dtype)], axis=0).astype(F32)
    row = lambda a: a.reshape(1, CONV_WIDTH).astype(F32)
    const = lambda shape: pl.BlockSpec(shape, lambda b, t: (0, 0))
    return pl.pallas_call(
        _conv_kernel,
        grid=(B, S // CONV_T),
        in_specs=[pl.BlockSpec((None, CONV_T, 1024), lambda b, t: (b, t, Z_CONV // 1024)),
                  const((CONV_K + 1, CONV_WIDTH)), const((1, CONV_WIDTH)), const((1, CONV_WIDTH)),
                  const((1, CONV_WIDTH))],
        out_specs=pl.BlockSpec((None, CONV_T, CONV_WIDTH), lambda b, t: (b, t, 0)),
        out_shape=jax.ShapeDtypeStruct((B, S, CONV_WIDTH), _ACT),
        scratch_shapes=[pltpu.VMEM((CONV_HALO + CONV_T, CONV_WIDTH), F32)],
        compiler_params=_cp("parallel", "arbitrary"),
        name="conformer_conv",
    )(z, w, row(conv_b), row(conv_g), row(conv_beta))


GLA_T = 256
_GLA_LEVELS = (1, 2, 4, 8, 16, 32)


def _gla_masks():
    t = np.arange(GLA_CHUNK)[:, None]
    s = np.arange(GLA_CHUNK)[None, :]
    ms = [(t == s)]
    for c in _GLA_LEVELS:
        ms.append(((t // c) % 2 == 1) & (s // c == t // c - 1))
    m = np.stack(ms).astype(np.float32)
    return np.tile(m, (1, GLA_HEADS, 1))


def _gla_kernel(gq_ref, gk_ref, gv_ref, sm_ref, gr_ref, gw_ref, gb_ref, gg_ref, tril_ref, lm_ref, hm_ref,
                o_ref, state):
    @pl.when(pl.program_id(1) == 0)
    def _():
        state[...] = jnp.zeros(state.shape, F32)

    C = GLA_CHUNK
    tidx = lax.broadcasted_iota(jnp.int32, (C, GLA_HEADS * GLA_DK), 0)

    def stack_heads(x):
        return jnp.concatenate([x * hm_ref[h:h + 1, :] for h in range(GLA_HEADS)], axis=0)

    def chunk(ci, carry):
        r0 = pl.multiple_of(ci * C, C)
        q = gq_ref[pl.ds(r0, C), :].astype(F32) * (GLA_DK ** -0.5)
        k = gk_ref[pl.ds(r0, C), :].astype(F32)
        v = gv_ref[pl.ds(r0, C), :]
        pre = _dot(sm_ref[pl.ds(r0, C), :], gw_ref[...]) + gb_ref[...]
        la = (jnp.minimum(pre, 0.0) - jnp.log1p(jnp.exp(-jnp.abs(pre)))) * (1.0 / GLA_TAU)
        b = _dot01_left(tril_ref[...], la)
        attn = _dot_nt(stack_heads(q), k) * lm_ref[0]
        bstart = b
        bnext = pltpu.roll(b, C - 1, axis=0)
        for li, c in enumerate(_GLA_LEVELS):
            odd = (tidx // c) % 2 == 1
            q_l = jnp.where(odd, q * jnp.exp(b - bstart), 0.0)
            k_l = jnp.where(odd, 0.0, k * jnp.exp(bnext - b))
            attn = attn + _dot_nt(stack_heads(q_l), k_l) * lm_ref[li + 1]
            half = (tidx % (2 * c)) < c
            bstart = jnp.where(half, bstart, pltpu.roll(bstart, c, axis=0))
            bnext = jnp.where(half, pltpu.roll(bnext, C - c, axis=0), bnext)
        st = state[...]
        r_intra = _dot(attn, v)
        r_inter = _dot_nt(stack_heads(q * jnp.exp(b)), st)
        b_last = b[C - 1:C, :]
        upd = _dot_tn(v, k * jnp.exp(b_last - b))
        new_st = st * jnp.exp(b_last) + jnp.concatenate(
            [upd[128 * h:128 * h + 128, 64 * h:64 * h + 64] for h in range(GLA_HEADS)], axis=1)
        state[...] = new_st
        outs = []
        for h in range(GLA_HEADS):
            o = r_intra[64 * h:64 * h + 64, 128 * h:128 * h + 128] + r_inter[64 * h:64 * h + 64, :]
            ms = jnp.mean(o * o, axis=-1, keepdims=True)
            outs.append(o * lax.rsqrt(ms + 1e-6) * gg_ref[...])
        gr = gr_ref[pl.ds(r0, C), :].astype(F32)
        o_ref[pl.ds(r0, C), :] = (jnp.concatenate(outs, axis=1) * (gr * _sigmoid(gr))).astype(o_ref.dtype)
        return carry

    lax.fori_loop(0, GLA_T // C, chunk, 0)


def _gla(z, gla_w, gla_b, gla_g):
    B, S, _ = z.shape
    gw = jnp.zeros((128, GLA_HEADS * GLA_DK), F32).at[SMALL_GA:SMALL_GA + GLA_RANK].set(gla_w).astype(_MM)
    tril = jnp.asarray(np.tril(np.ones((GLA_CHUNK, GLA_CHUNK), np.float32)), dtype=_MM)
    lm = jnp.asarray(_gla_masks())
    hm = jnp.asarray(np.repeat(np.eye(GLA_HEADS, dtype=np.float32), GLA_DK, axis=1))
    const = lambda shape: pl.BlockSpec(shape, lambda b, t: (0,) * len(shape))
    zspec = lambda w, off: pl.BlockSpec((None, GLA_T, w), lambda b, t: (b, t, off // w))
    return pl.pallas_call(
        _gla_kernel,
        grid=(B, S // GLA_T),
        in_specs=[zspec(256, Z_GQ), zspec(256, Z_GK), zspec(512, Z_GV), zspec(128, Z_SMALL), zspec(512, Z_GR),
                  const((128, 256)), const((1, 256)), const((1, GLA_DV)), const((GLA_CHUNK, GLA_CHUNK)),
                  const((7, GLA_HEADS * GLA_CHUNK, GLA_CHUNK)), const((GLA_HEADS, 256))],
        out_specs=pl.BlockSpec((None, GLA_T, 512), lambda b, t: (b, t, 0)),
        out_shape=jax.ShapeDtypeStruct((B, S, 512), _ACT),
        scratch_shapes=[pltpu.VMEM((GLA_DV, GLA_HEADS * GLA_DK), F32)],
        compiler_params=_cp("parallel", "arbitrary"),
        name="gla",
    )(z, z, z, z, z, gw, gla_b.reshape(1, -1).astype(F32), gla_g.reshape(1, -1).astype(F32), tril, lm, hm)


MERGE_T = 512


def _merge_kernel(oa_ref, ob_ref, oc_ref, mg_ref, wb_ref, wo_ref, x_ref, g_ref, b_ref, o_ref):
    merged = None
    for j, r in enumerate((oa_ref, ob_ref, oc_ref)):
        gate = _sigmoid(mg_ref[:, D_MODEL * j:D_MODEL * (j + 1)].astype(F32))
        term = gate * _dot(r[...], wb_ref[j])
        merged = term if merged is None else merged + term
    mix = _dot(merged, wo_ref[...])
    o_ref[...] = _layer_norm(DEEPNORM_ALPHA * x_ref[...] + mix, g_ref[...], b_ref[...])


def _merge(oa, ob, oc, z2, w_branch, w_out, x2, g, b):
    n = x2.shape[0]
    tok = lambda w, blk=0: pl.BlockSpec((MERGE_T, w), lambda i: (i, blk))
    const = lambda shape: pl.BlockSpec(shape, lambda i: (0,) * len(shape))
    return pl.pallas_call(
        _merge_kernel,
        grid=(n // MERGE_T,),
        in_specs=[tok(512), tok(512), tok(512), tok(3072, Z_MERGE // 3072), const((3, 512, D_MODEL)),
                  const((D_MODEL, D_MODEL)), tok(D_MODEL), const((1, D_MODEL)), const((1, D_MODEL))],
        out_specs=tok(D_MODEL),
        out_shape=jax.ShapeDtypeStruct((n, D_MODEL), F32),
        compiler_params=_cp("parallel"),
        name="merge_outproj_ln",
    )(oa, ob, oc, z2, w_branch.astype(_MM), w_out.astype(_MM), x2, g.reshape(1, -1), b.reshape(1, -1))


XA_T = 512


def _xattn_kernel(x_ref, kv_ref, wq_ref, wo_ref, g_ref, b_ref, o_ref, o3_ref):
    x = x_ref[...]
    q = (_dot(x, wq_ref[...]) * (XA_DH ** -0.5)).astype(_MM)
    heads = []
    for h in range(XA_HEADS):
        hs = slice(XA_DH * h, XA_DH * (h + 1))
        s = _dot_nt(q[:, hs], kv_ref[:, hs])
        e = jnp.exp(s - jnp.max(s, axis=-1, keepdims=True))
        p = e / jnp.sum(e, axis=-1, keepdims=True)
        heads.append(_dot(p, kv_ref[:, D_MODEL + XA_DH * h:D_MODEL + XA_DH * (h + 1)]))
    xa = _dot(jnp.concatenate(heads, axis=1), wo_ref[...])
    y = _layer_norm(DEEPNORM_ALPHA * x + xa, g_ref[...], b_ref[...])
    o_ref[...] = y
    o3_ref[:, 0, :] = y


def _cross_attention(x, kv, wq, wo, g, b):
    B, S, _ = x.shape
    M = kv.shape[1]
    nt = S // XA_T
    const = lambda shape: pl.BlockSpec(shape, lambda bb, t: (0,) * len(shape))
    return pl.pallas_call(
        _xattn_kernel,
        grid=(B, nt),
        in_specs=[pl.BlockSpec((None, XA_T, D_MODEL), lambda bb, t: (bb, t, 0)),
                  pl.BlockSpec((None, M, 2 * D_MODEL), lambda bb, t: (bb, 0, 0)),
                  const((D_MODEL, D_MODEL)), const((D_MODEL, D_MODEL)), const((1, D_MODEL)), const((1, D_MODEL))],
        out_specs=[pl.BlockSpec((None, XA_T, D_MODEL), lambda bb, t: (bb, t, 0)),
                   pl.BlockSpec((XA_T, 1, D_MODEL), lambda bb, t: (bb * nt + t, 0, 0))],
        out_shape=[jax.ShapeDtypeStruct((B, S, D_MODEL), F32), jax.ShapeDtypeStruct((B * S, 1, D_MODEL), F32)],
        compiler_params=_cp("parallel", "parallel"),
        name="cross_attention_ln",
    )(x, kv, wq.astype(_MM), wo.astype(_MM), g.reshape(1, -1), b.reshape(1, -1))


ROUTE_T = 512


def _router_kernel(x_ref, wt_ref, b_ref, upper_ref, ti_ref, gate_ref, rank_ref, cnt_ref, carry):
    @pl.when(pl.program_id(0) == 0)
    def _():
        carry[...] = jnp.zeros(carry.shape, F32)

    x = x_ref[...]
    xh = x.astype(_MM)
    xl = (x - xh.astype(F32)).astype(_MM)
    w = wt_ref[...]
    wh = w.astype(_MM)
    wl = (w - wh.astype(F32)).astype(_MM)
    logits = _dot_nt(wh, xh) + _dot_nt(wh, xl) + _dot_nt(wl, xh) + b_ref[...]
    eidx = lax.broadcasted_iota(jnp.int32, logits.shape, 0)
    v = logits
    tops, idxs, hots = [], [], []
    for _ in range(TOP_K):
        m = jnp.max(v, axis=0, keepdims=True)
        idx = jnp.min(jnp.where(v == m, eidx, N_EXPERTS), axis=0, keepdims=True)
        hot = eidx == idx
        v = jnp.where(hot, -jnp.inf, v)
        tops.append(m)
        idxs.append(idx)
        hots.append(jnp.where(hot, 1.0, 0.0))
    es = [jnp.exp(t - tops[0]) for t in tops]
    den = es[0] + es[1] + es[2] + es[3]
    multihot = hots[0] + hots[1] + hots[2] + hots[3]
    before = jnp.dot(multihot.astype(_MM), upper_ref[...], preferred_element_type=F32) + carry[...][:, 0:1]
    ranks = [jnp.sum(h * before, axis=0, keepdims=True) for h in hots]
    pad = jnp.zeros((8 - TOP_K, x.shape[0]), F32)
    ti_ref[...] = jnp.concatenate(idxs + [pad.astype(jnp.int32)], axis=0)
    gate_ref[...] = jnp.concatenate([e / den for e in es] + [pad], axis=0)
    rank_ref[...] = jnp.concatenate(ranks + [pad], axis=0).astype(jnp.int32)
    carry[...] = carry[...] + jnp.sum(multihot, axis=1, keepdims=True)
    cnt_ref[...] = carry[...]


def _router(x2, router_w, router_b):
    n = x2.shape[0]
    upper = jnp.asarray(np.triu(np.ones((ROUTE_T, ROUTE_T), np.float32), 1), dtype=_MM)
    bcol = jnp.broadcast_to(router_b.astype(F32)[:, None], (N_EXPERTS, ROUTE_T))
    tokspec = pl.BlockSpec((8, ROUTE_T), lambda i: (0, i))
    const = lambda shape: pl.BlockSpec(shape, lambda i: (0,) * len(shape))
    return pl.pallas_call(
        _router_kernel,
        grid=(n // ROUTE_T,),
        in_specs=[pl.BlockSpec((ROUTE_T, D_MODEL), lambda i: (i, 0)), const((N_EXPERTS, D_MODEL)),
                  const((N_EXPERTS, ROUTE_T)), const((ROUTE_T, ROUTE_T))],
        out_specs=[tokspec, tokspec, tokspec, const((N_EXPERTS, 128))],
        out_shape=[jax.ShapeDtypeStruct((8, n), jnp.int32), jax.ShapeDtypeStruct((8, n), F32),
                   jax.ShapeDtypeStruct((8, n), jnp.int32), jax.ShapeDtypeStruct((N_EXPERTS, 128), F32)],
        scratch_shapes=[pltpu.VMEM((N_EXPERTS, 128), F32)],
        compiler_params=_cp("arbitrary"),
        name="moe_router",
    )(x2, router_w.T.astype(F32), bcol, upper)


def _moe_kernel(blk_e_ref, nvalid_ref, tokc_ref, tokn_ref, dstp_ref, x_hbm, wgu_ref, bgu_ref, wdn_ref, bdn_ref,
                y_hbm, xbuf, ybuf, sem_in, sem_out):
    b = pl.program_id(0)
    nvalid = nvalid_ref[0]
    slot = b % 2
    other = 1 - slot

    def gather_row(tok_ref, s, r):
        pltpu.make_async_copy(x_hbm.at[tok_ref[0, r]], xbuf.at[s, pl.ds(r, 1), :], sem_in.at[s]).start()

    def scatter_row(s, r):
        pltpu.make_async_copy(ybuf.at[s, pl.ds(r, 1), :], y_hbm.at[dstp_ref[0, r]], sem_out.at[s]).start()

    def wait_in(s):
        pltpu.make_async_copy(x_hbm.at[pl.ds(0, MOE_BLK), 0], xbuf.at[s], sem_in.at[s]).wait()

    def wait_out(s):
        pltpu.make_async_copy(ybuf.at[s], y_hbm.at[pl.ds(0, MOE_BLK), 0], sem_out.at[s]).wait()

    @pl.when(b == 0)
    def _():
        ybuf[...] = jnp.zeros(ybuf.shape, F32)

        def body(r, c):
            gather_row(tokc_ref, 0, r)
            return c
        lax.fori_loop(0, MOE_BLK, body, 0)

    @pl.when(jnp.logical_and(b >= 1, b <= nvalid))
    def _():
        wait_out(slot)

    @pl.when(b < nvalid)
    def _():
        wait_in(slot)
        xb = xbuf[slot].astype(_MM)
        for r in range(MOE_BLK):
            gather_row(tokn_ref, other, r)
        gu = jnp.dot(xb, wgu_ref[...], preferred_element_type=F32) + bgu_ref[...]
        for r in range(MOE_BLK):
            scatter_row(other, r)
        g = jnp.minimum(gu[:, :D_MODEL], SWIGLU_LIMIT)
        u = jnp.clip(gu[:, D_MODEL:], -SWIGLU_LIMIT, SWIGLU_LIMIT)
        h = (u + 1.0) * g * _sigmoid(SWIGLU_ALPHA * g)
        ybuf[slot] = _dot(h, wdn_ref[...]) + bdn_ref[...]

    @pl.when(b == nvalid)
    def _():
        def body(r, c):
            scatter_row(other, r)
            return c
        lax.fori_loop(0, MOE_BLK, body, 0)
        wait_out(other)
        wait_in(slot)


def _moe_experts(x2, slot_tok, slot_dst, blk_e, nvalid, n_out_rows, w_gu, b_gu, w_dn, b_dn):
    n_blocks = blk_e.shape[0]
    tok3 = slot_tok.reshape(n_blocks, 1, MOE_BLK)
    dst3 = slot_dst.reshape(n_blocks + 1, 1, MOE_BLK)
    smem = lambda f: pl.BlockSpec((None, 1, MOE_BLK), f, memory_space=pltpu.SMEM)
    grid_spec = pltpu.PrefetchScalarGridSpec(
        num_scalar_prefetch=2,
        grid=(n_blocks,),
        in_specs=[
            smem(lambda b, be, nv: (b, 0, 0)),
            smem(lambda b, be, nv: (jnp.minimum(b + 1, n_blocks - 1), 0, 0)),
            smem(lambda b, be, nv: (b, 0, 0)),
            pl.BlockSpec(memory_space=pl.ANY),
            pl.BlockSpec((None, D_MODEL, 2 * D_MODEL), lambda b, be, nv: (be[b], 0, 0)),
            pl.BlockSpec((None, 1, 2 * D_MODEL), lambda b, be, nv: (be[b], 0, 0)),
            pl.BlockSpec((None, D_MODEL, D_MODEL), lambda b, be, nv: (be[b], 0, 0)),
            pl.BlockSpec((None, 1, D_MODEL), lambda b, be, nv: (be[b], 0, 0)),
        ],
        out_specs=pl.BlockSpec(memory_space=pl.ANY),
        scratch_shapes=[pltpu.VMEM((2, MOE_BLK, D_MODEL), F32), pltpu.VMEM((2, MOE_BLK, D_MODEL), F32),
                        pltpu.SemaphoreType.DMA((2,)), pltpu.SemaphoreType.DMA((2,))],
    )
    return pl.pallas_call(
        _moe_kernel,
        grid_spec=grid_spec,
        out_shape=jax.ShapeDtypeStruct((n_out_rows, 1, D_MODEL), F32),
        compiler_params=_cp("arbitrary"),
        name="moe_experts",
    )(blk_e, nvalid, tok3, tok3, dst3, x2, w_gu.astype(_MM), b_gu.reshape(N_EXPERTS, 1, -1), w_dn.astype(_MM),
      b_dn.reshape(N_EXPERTS, 1, -1))


def _combine_kernel(y0_ref, y1_ref, y2_ref, y3_ref, gate_ref, x_ref, g_ref, b_ref, o_ref):
    gate = gate_ref[...]
    ff = gate[:, 0:1] * y0_ref[:, 0, :]
    for k, r in enumerate((y1_ref, y2_ref, y3_ref)):
        ff = ff + gate[:, k + 1:k + 2] * r[:, 0, :]
    o_ref[...] = _layer_norm(DEEPNORM_ALPHA * x_ref[...] + ff, g_ref[...], b_ref[...])


def _moe_combine(ys, gate_t, x2, g, b):
    n = x2.shape[0]
    steps = n // COMB_T
    const = lambda shape: pl.BlockSpec(shape, lambda i: (0,) * len(shape))
    yspec = lambda k: pl.BlockSpec((COMB_T, 1, D_MODEL), lambda i: (k * steps + i, 0, 0))
    return pl.pallas_call(
        _combine_kernel,
        grid=(steps,),
        in_specs=[yspec(0), yspec(1), yspec(2), yspec(3), pl.BlockSpec((COMB_T, 8), lambda i: (i, 0)),
                  pl.BlockSpec((COMB_T, D_MODEL), lambda i: (i, 0)), const((1, D_MODEL)), const((1, D_MODEL))],
        out_specs=pl.BlockSpec((COMB_T, D_MODEL), lambda i: (i, 0)),
        out_shape=jax.ShapeDtypeStruct((n, D_MODEL), F32),
        compiler_params=_cp("parallel"),
        name="moe_combine_ln",
    )(ys, ys, ys, ys, gate_t, x2, g.reshape(1, -1), b.reshape(1, -1))


def _moe_ffn(x2, x3, router_w, router_b, w_gu, b_gu, w_dn, b_dn, g, b):
    n = x2.shape[0]
    a = n * TOP_K
    top_i, gate, rank, cnt = _router(x2, router_w, router_b)
    counts = cnt[:, 0].astype(jnp.int32)
    padded = (counts + MOE_BLK - 1) // MOE_BLK * MOE_BLK
    pends = jnp.cumsum(padded)
    pstarts = pends - padded
    n_blocks = a // MOE_BLK + N_EXPERTS
    n_slots = n_blocks * MOE_BLK
    eids = jnp.arange(N_EXPERTS, dtype=jnp.int32)
    hot = top_i[:TOP_K, None, :] == eids[None, :, None]
    dest = jnp.sum(jnp.where(hot, pstarts[None, :, None], 0), axis=1) + rank[:TOP_K]
    spare = a + jnp.arange(n_slots, dtype=jnp.int32) % MOE_BLK
    slot_asg = spare.at[dest.reshape(-1)].set(jnp.arange(a, dtype=jnp.int32))
    slot_tok = jnp.where(slot_asg < a, slot_asg % n, 0)
    slot_dst = jnp.concatenate([spare[:MOE_BLK], slot_asg])
    blk_start = jnp.arange(n_blocks, dtype=jnp.int32) * MOE_BLK
    blk_e = jnp.minimum(jnp.sum((pends[None, :] <= blk_start[:, None]).astype(jnp.int32), axis=1), N_EXPERTS - 1)
    nvalid = (pends[-1:] // MOE_BLK).astype(jnp.int32)
    ys = _moe_experts(x3, slot_tok, slot_dst, blk_e, nvalid, a + MOE_BLK, w_gu, b_gu, w_dn, b_dn)
    return _moe_combine(ys, gate.T, x2, g, b)


def _layer(x, mem, tables, w_in, cmp_pe, cmp_w1, cmp_b1, cmp_w2, conv_w, conv_b, conv_g, conv_beta, gla_w, gla_b,
           gla_g, w_branch, w_out, xa_wq, xa_wkv, xa_wo, router_w, router_b, w_gu, b_gu, w_dn, b_dn, ng, nb):
    B, S, D = x.shape
    n = B * S
    x2 = x.reshape(n, D)
    z2 = _matmul(x2, _reorder_w_in(w_in), 1024, ZW // 5, _ACT)
    z = z2.reshape(B, S, ZW)
    kvc = z[:, :, Z_KV:Z_KV + 256].reshape(B, S // 16, 16, 2, 128).transpose(3, 0, 1, 2, 4)
    cmp = _compress(kvc.reshape(2, B, S // 16, 2048), cmp_pe, cmp_w1, cmp_b1, cmp_w2)
    kv4 = jnp.pad(z[:, :, Z_KV + 256:Z_KV + 768], ((0, 0), (WINDOW, 0), (0, 0)))
    oa = _nsa_attention(z, cmp[0], cmp[1], kv4, tables)
    ob = _conformer_conv(z, conv_w, conv_b, conv_g, conv_beta)
    oc = _gla(z, gla_w, gla_b, gla_g)
    x2 = _merge(oa.reshape(n, 512), ob.reshape(n, 512), oc.reshape(n, 512), z2, w_branch, w_out, x2, ng[0], nb[0])
    kv = _matmul(mem.reshape(-1, D), xa_wkv.astype(_MM), 1024, 1024, _ACT).reshape(B, -1, 2 * D)
    x2, x3 = _cross_attention(x2.reshape(B, S, D), kv, xa_wq, xa_wo, ng[1], nb[1])
    x2 = _moe_ffn(x2.reshape(n, D), x3, router_w, router_b, w_gu, b_gu, w_dn, b_dn, ng[2], nb[2])
    return x2.reshape(B, S, D)


def kernel(x, mem, rel_bias, w_in, cmp_pe, cmp_w1, cmp_b1, cmp_w2, conv_w, conv_b, conv_norm_g, conv_norm_b,
           gla_gate_w, gla_gate_b, gla_norm_g, w_branch, w_out, xa_wq, xa_wkv, xa_wo, router_w, router_b,
           expert_w_gu, expert_b_gu, expert_w_down, expert_b_down, norm_g, norm_b):
    tables = _nsa_tables(rel_bias)
    for l in range(DEPTH):
        x = _layer(x, mem, tables, w_in[l], cmp_pe[l], cmp_w1[l], cmp_b1[l], cmp_w2[l], conv_w[l], conv_b[l],
                   conv_norm_g[l], conv_norm_b[l], gla_gate_w[l], gla_gate_b[l], gla_norm_g[l], w_branch[l],
                   w_out[l], xa_wq[l], xa_wkv[l], xa_wo[l], router_w[l], router_b[l], expert_w_gu[l],
                   expert_b_gu[l], expert_w_down[l], expert_b_down[l], norm_g[l], norm_b[l])
    return x
```

```python
import functools
import math

import numpy as np
import jax
import jax.numpy as jnp
from jax import lax
from jax.experimental import pallas as pl
from jax.experimental.pallas import tpu as pltpu

F32 = jnp.float32
_MM = jnp.bfloat16
_ACT = jnp.bfloat16
NEG = -1e30
LOG2E = 1.4426950408889634

D_MODEL = 1024
DEPTH = 2
MEM_LEN = 256
NSA_HEADS = 8
NSA_DH = 64
CMP_LEN = 32
CMP_STRIDE = 16
SEL_BLOCK = 64
SEL_TOP = 16
WINDOW = 512
CONV_WIDTH = 512
CONV_K = 31
GLA_HEADS = 4
GLA_DK = 64
GLA_DV = 128
GLA_RANK = 16
GLA_TAU = 16.0
GLA_CHUNK = 64
REL_BUCKETS = 32
REL_MAX_DIST = 128
XA_HEADS = 4
XA_DH = 256
N_EXPERTS = 32
TOP_K = 4
SWIGLU_ALPHA = 1.702
SWIGLU_LIMIT = 7.0
DEEPNORM_ALPHA = (2 * DEPTH) ** 0.25

Z_MERGE = 0
Z_CONV = 3072
Z_Q = 4096
Z_GV = 4608
Z_GR = 5120
Z_KV = 5632
Z_GQ = 6400
Z_GK = 6656
Z_SMALL = 6912
ZW = 7040
SMALL_GA = 24

_IN_SIZES = (512, 768, 24, 1024, 256, 256, 512, 16, 512, 3072)
_IN_OFF = np.concatenate([[0], np.cumsum(_IN_SIZES)]).astype(int)

VMEM_LIMIT = 56 * 1024 * 1024

NSA_QB = 128
MOE_BLK = 512
COMB_T = 256


def _cp(*sem):
    return pltpu.CompilerParams(dimension_semantics=sem, vmem_limit_bytes=VMEM_LIMIT)


def _dot(a, b):
    return jnp.dot(a.astype(_MM), b.astype(_MM), preferred_element_type=F32)


def _dot_nt(a, b):
    return lax.dot_general(a.astype(_MM), b.astype(_MM), (((1,), (1,)), ((), ())), preferred_element_type=F32)


def _dot_tn(a, b):
    return lax.dot_general(a.astype(_MM), b.astype(_MM), (((0,), (0,)), ((), ())), preferred_element_type=F32)


def _split3(x):
    x1 = x.astype(_MM)
    r1 = x - x1.astype(F32)
    x2 = r1.astype(_MM)
    x3 = (r1 - x2.astype(F32)).astype(_MM)
    return x1, x2, x3


def _dot01_left(m01, x):
    x1, x2, x3 = _split3(x)
    return (jnp.dot(m01, x1, preferred_element_type=F32) + jnp.dot(m01, x2, preferred_element_type=F32)
            + jnp.dot(m01, x3, preferred_element_type=F32))


def _layer_norm(y, g, b, eps=1e-5):
    mu = jnp.mean(y, axis=-1, keepdims=True)
    d = y - mu
    var = jnp.mean(d * d, axis=-1, keepdims=True)
    return d * lax.rsqrt(var + eps) * g + b


def _sigmoid(x):
    return 1.0 / (1.0 + jnp.exp(-x))


def _mm_kernel(x_ref, w_ref, o_ref):
    o_ref[...] = _dot(x_ref[...], w_ref[...]).astype(o_ref.dtype)


def _matmul(x, w, tm, tn, out_dtype):
    m, k = x.shape
    n = w.shape[1]
    tm = min(tm, m)
    return pl.pallas_call(
        _mm_kernel,
        grid=(m // tm, n // tn),
        in_specs=[pl.BlockSpec((tm, k), lambda i, j: (i, 0)), pl.BlockSpec((k, tn), lambda i, j: (0, j))],
        out_specs=pl.BlockSpec((tm, tn), lambda i, j: (i, j)),
        out_shape=jax.ShapeDtypeStruct((m, n), out_dtype),
        compiler_params=_cp("parallel", "parallel"),
        name="matmul",
    )(x, w)


def _reorder_w_in(w_in):
    def cols(i):
        return w_in[:, _IN_OFF[i]:_IN_OFF[i + 1]]
    small = jnp.concatenate([cols(2), cols(7), jnp.zeros((D_MODEL, 128 - 40), w_in.dtype)], axis=1)
    w = jnp.concatenate([cols(9), cols(3), cols(0), cols(6), cols(8), cols(1), cols(4), cols(5), small], axis=1)
    return w.astype(_MM)


def _cmp_kernel(x_ref, pea_ref, peb_ref, wa_ref, wb_ref, b1_ref, w2_ref, o_ref):
    x = x_ref[...].astype(F32)
    a = _dot(x + pea_ref[...], wa_ref[...])
    bm = _dot(x + peb_ref[...], wb_ref[...])
    n = bm.shape[0]
    bs = pltpu.roll(bm, n - 1, axis=0)
    h = jax.nn.gelu(a + bs + b1_ref[...], approximate=True)
    o_ref[...] = _dot(h, w2_ref[...]).astype(o_ref.dtype)


def _compress(kv2, cmp_pe, cmp_w1, cmp_b1, cmp_w2):
    _, B, R, _ = kv2.shape
    eye2 = jnp.eye(2, dtype=F32)

    def half(w1h):
        w = jnp.einsum('rlde,gh->rlgdhe', w1h, eye2)
        return w.reshape(2, 16 * 128, 128).astype(_MM)

    wa = half(cmp_w1[:, :16])
    wb = half(cmp_w1[:, 16:])

    def pe_half(p):
        return jnp.broadcast_to(p[:, :, None, :], (2, 16, 2, 64)).reshape(2, 1, 2048).astype(F32)

    pea = pe_half(cmp_pe[:, :16])
    peb = pe_half(cmp_pe[:, 16:])
    b1 = jnp.tile(cmp_b1, (1, 2)).reshape(2, 1, 128).astype(F32)
    w2 = jnp.einsum('rde,gh->rgdhe', cmp_w2, eye2).reshape(2, 128, 128).astype(_MM)
    return pl.pallas_call(
        _cmp_kernel,
        grid=(2, B),
        in_specs=[
            pl.BlockSpec((None, None, R, 2048), lambda r, b: (r, b, 0, 0)),
            pl.BlockSpec((None, 1, 2048), lambda r, b: (r, 0, 0)),
            pl.BlockSpec((None, 1, 2048), lambda r, b: (r, 0, 0)),
            pl.BlockSpec((None, 2048, 128), lambda r, b: (r, 0, 0)),
            pl.BlockSpec((None, 2048, 128), lambda r, b: (r, 0, 0)),
            pl.BlockSpec((None, 1, 128), lambda r, b: (r, 0, 0)),
            pl.BlockSpec((None, 128, 128), lambda r, b: (r, 0, 0)),
        ],
        out_specs=pl.BlockSpec((None, None, R, 128), lambda r, b: (r, b, 0, 0)),
        out_shape=jax.ShapeDtypeStruct((2, B, R, 128), _ACT),
        compiler_params=_cp("parallel", "parallel"),
        name="nsa_compress",
    )(kv2, pea, peb, wa, wb, b1, w2)


def _t5_bucket_np(dist):
    n = np.maximum(dist, 0)
    exact = REL_BUCKETS // 2
    lr = np.log(np.maximum(n, 1).astype(np.float32) / np.float32(exact)) / np.float32(math.log(REL_MAX_DIST / exact))
    large = exact + (lr * np.float32(REL_BUCKETS - exact)).astype(np.int32)
    return np.where(n < exact, n, np.minimum(large, REL_BUCKETS - 1))


def _nsa_tables(rel_bias):
    rel = (rel_bias - rel_bias[REL_BUCKETS - 1:REL_BUCKETS]).astype(F32) * LOG2E
    q = np.arange(NSA_QB)

    def table(dist, valid, fill):
        hot = jnp.asarray(_t5_bucket_np(dist), jnp.int32)[:, :, None] == jnp.arange(REL_BUCKETS)[None, None, :]
        t = jnp.sum(jnp.where(hot[..., None], rel[None, None], 0.0), axis=2)
        t = jnp.where(valid[:, :, None], t, fill)
        return jnp.transpose(t, (0, 2, 1)).reshape(dist.shape[0], NSA_HEADS * NSA_QB)

    cc = np.arange(24) - 16
    d_c = q[None, :] - CMP_STRIDE * cc[:, None] - (CMP_LEN - 1)
    t_cmp = table(d_c, d_c >= 0, 0.0)
    ko = np.arange(256)
    d_n = q[None, :] + 128 - ko[:, None]
    t_near = table(d_n, d_n >= 0, NEG)
    kw = np.arange(WINDOW + NSA_QB)
    d_w = q[None, :] + WINDOW - kw[:, None]
    in_win = (d_w >= 0) & (d_w < WINDOW)
    t_win = jnp.stack([table(d_w, in_win & (kw[:, None] >= WINDOW - NSA_QB * v), NEG) for v in range(5)])
    return t_cmp, t_near, t_win


def _nsa_kernel(q_ref, gate_ref, kcmp_ref, vcmp_ref, ksel_ref, vsel_ref, kwin_ref, vwin_ref,
                cmpa_ref, tcmp_ref, tnear_ref, twin_ref, ovl_ref, out_ref, s_scr, v_scr, sel_scr, far_scr):
    i = pl.program_id(1)
    ncmp = kcmp_ref.shape[0]
    nblk = ovl_ref.shape[0]
    qt = (q_ref[...].astype(F32) * (NSA_DH ** -0.5 * LOG2E)).T
    gt = _sigmoid(gate_ref[...].astype(F32)).T
    zeros64 = jnp.zeros((64, NSA_QB), F32)
    jidx = lax.broadcasted_iota(jnp.int32, (nblk, NSA_QB), 0)
    qidx = lax.broadcasted_iota(jnp.int32, (nblk, NSA_QB), 1)
    sub8 = lax.broadcasted_iota(jnp.int32, (8, NSA_QB), 0)
    cur = 2 * i + (qidx >= SEL_BLOCK).astype(jnp.int32)
    forced = (jidx == 0) | (jidx == cur) | (jidx == cur - 1)
    future = jidx > cur
    neg8 = jnp.full((8, NSA_QB), NEG, F32)
    s_scr[0:16, :] = jnp.zeros((16, 4 * NSA_QB), F32)

    def mask_rows(scr, first_block, n):
        rows = [jnp.broadcast_to(scr[pl.ds(8 + first_block + b, 1), :], (SEL_BLOCK, NSA_QB)) for b in range(n)]
        return jnp.concatenate(rows, axis=0)

    def add_mask(s, m):
        return jnp.concatenate([s[:, 128 * h:128 * h + 128] + m for h in range(4)], axis=1)

    q_gs, o_cs, o_ws, state = [], [], [], []
    for g in range(2):
        blocks = []
        for h in range(4):
            r = qt[64 * (4 * g + h):64 * (4 * g + h) + 64]
            blocks.append(jnp.concatenate([r, zeros64] if g == 0 else [zeros64, r], axis=0))
        q_g = jnp.concatenate(blocks, axis=1).astype(_MM)
        q_gs.append(q_g)
        cs = slice(512 * g, 512 * g + 512)

        s_scr[16:16 + ncmp, :] = _dot(kcmp_ref[...], q_g)
        w0 = pl.multiple_of(8 * i, 8)
        s_scr[pl.ds(w0, 24), :] = s_scr[pl.ds(w0, 24), :] + tcmp_ref[:, cs]
        sc = jnp.where(cmpa_ref[...] <= NSA_QB * i, s_scr[16:16 + ncmp, :], NEG)
        m = jnp.maximum(jnp.max(sc, axis=0, keepdims=True), 0.1 * NEG)
        e = jnp.exp2(sc - m)
        l = jnp.sum(e, axis=0, keepdims=True)
        pc = e / jnp.maximum(l, 1e-30)
        o_cs.append(_dot_tn(vcmp_ref[...], pc))

        psum = pc[:, 0:128] + pc[:, 128:256] + pc[:, 256:384] + pc[:, 384:512]
        imp = _dot01_left(ovl_ref[...], psum)
        v = jnp.where(forced, 1e30, jnp.where(future, -1.0, imp))
        v_scr[g] = v
        sel_scr[g, 0:8, :] = neg8
        far_scr[g, 0:8, :] = neg8
        for r8 in range(0, nblk, 8):
            vr = v[r8:r8 + 8]
            cnt = jnp.zeros((8, NSA_QB), F32)
            for jp in range(nblk):
                row = jnp.broadcast_to(v_scr[g, pl.ds(jp, 1), :], (8, NSA_QB))
                if jp < r8:
                    cnt = cnt + jnp.where(row >= vr, 1.0, 0.0)
                elif jp >= r8 + 8:
                    cnt = cnt + jnp.where(row > vr, 1.0, 0.0)
                else:
                    cnt = cnt + jnp.where(sub8 > jp - r8, jnp.where(row >= vr, 1.0, 0.0),
                                          jnp.where(row > vr, 1.0, 0.0))
            sel = jnp.where(cnt < float(min(SEL_TOP, nblk)), 0.0, NEG)
            sel_scr[g, 8 + r8:16 + r8, :] = sel
            far_scr[g, 8 + r8:16 + r8, :] = jnp.where(jidx[r8:r8 + 8] <= 2 * i - 3, sel, NEG)

        n0 = pl.multiple_of(NSA_QB * i + WINDOW - 128, 128)
        sn = add_mask(_dot(ksel_ref[pl.ds(n0, 256), :], q_g) + tnear_ref[:, cs], mask_rows(sel_scr.at[g], 2 * i - 2, 4))
        m_s = jnp.max(sn, axis=0, keepdims=True)
        e = jnp.exp2(sn - m_s)
        state += [m_s, jnp.sum(e, axis=0, keepdims=True), _dot_tn(vsel_ref[pl.ds(n0, 256), :], e)]

        w_start = pl.multiple_of(NSA_QB * i, 128)
        sw = _dot(kwin_ref[pl.ds(w_start, WINDOW + NSA_QB), :], q_g) + twin_ref[:, cs]
        m_w = jnp.max(sw, axis=0, keepdims=True)
        e = jnp.exp2(sw - m_w)
        l_w = jnp.sum(e, axis=0, keepdims=True)
        o_ws.append(_dot_tn(vwin_ref[pl.ds(w_start, WINDOW + NSA_QB), :], e) / l_w)

    def far_body(c, carry):
        k0 = pl.multiple_of(WINDOW + 512 * c, 512)
        kf = ksel_ref[pl.ds(k0, 512), :]
        vf = vsel_ref[pl.ds(k0, 512), :]
        new = []
        for g in range(2):
            m_o, l_o, a_o = carry[3 * g:3 * g + 3]
            sf = add_mask(_dot(kf, q_gs[g]), mask_rows(far_scr.at[g], 8 * c, 8))
            m_n = jnp.maximum(m_o, jnp.max(sf, axis=0, keepdims=True))
            alpha = jnp.exp2(m_o - m_n)
            ef = jnp.exp2(sf - m_n)
            new += [m_n, alpha * l_o + jnp.sum(ef, axis=0, keepdims=True), alpha * a_o + _dot_tn(vf, ef)]
        return tuple(new)

    state = lax.fori_loop(0, (2 * i + 5) // 8, far_body, tuple(state))

    out_rows = []
    for g in range(2):
        o_s = state[3 * g + 2] / state[3 * g + 1]
        for h in range(4):
            hs = slice(128 * h, 128 * h + 128)
            ds_ = slice(64 * g, 64 * g + 64)
            gi = (4 * g + h) * 3
            out_rows.append(gt[gi:gi + 1, :] * o_cs[g][ds_, hs] + gt[gi + 1:gi + 2, :] * o_s[ds_, hs]
                            + gt[gi + 2:gi + 3, :] * o_ws[g][ds_, hs])
    out_ref[...] = jnp.concatenate(out_rows, axis=0).T.astype(out_ref.dtype)


def _nsa_attention(z, kcmp, vcmp, kv4, tables):
    B, S, _ = z.shape
    n_steps = S // NSA_QB
    nblk = S // SEL_BLOCK
    ncmp = kcmp.shape[1]
    t_cmp, t_near, t_win = tables
    cmpa = (CMP_STRIDE * np.arange(ncmp)[:, None] + (CMP_LEN - 1) - np.arange(NSA_QB)[None, :]).astype(np.int32)
    cmpa = jnp.asarray(np.tile(cmpa, (1, 4)))
    c = np.arange(ncmp)[None, :]
    j = np.arange(nblk)[:, None]
    ovl = ((CMP_STRIDE * c < (j + 1) * SEL_BLOCK) & (CMP_STRIDE * c + CMP_LEN > j * SEL_BLOCK)
           & (c < (S - CMP_LEN) // CMP_STRIDE + 1))
    ovl = jnp.asarray(ovl.astype(np.float32), dtype=_MM)
    sp = S + WINDOW
    const = lambda shape: pl.BlockSpec(shape, lambda b, i: (0,) * len(shape))
    return pl.pallas_call(
        _nsa_kernel,
        grid=(B, n_steps),
        in_specs=[
            pl.BlockSpec((None, NSA_QB, 512), lambda b, i: (b, i, Z_Q // 512)),
            pl.BlockSpec((None, NSA_QB, 128), lambda b, i: (b, i, Z_SMALL // 128)),
            pl.BlockSpec((None, ncmp, 128), lambda b, i: (b, 0, 0)),
            pl.BlockSpec((None, ncmp, 128), lambda b, i: (b, 0, 0)),
            pl.BlockSpec((None, sp, 128), lambda b, i: (b, 0, 0)),
            pl.BlockSpec((None, sp, 128), lambda b, i: (b, 0, 1)),
            pl.BlockSpec((None, sp, 128), lambda b, i: (b, 0, 2)),
            pl.BlockSpec((None, sp, 128), lambda b, i: (b, 0, 3)),
            const((ncmp, 512)), const((24, 1024)), const((256, 1024)),
            pl.BlockSpec((None, WINDOW + NSA_QB, 1024), lambda b, i: (jnp.minimum(i, 4), 0, 0)),
            const((nblk, ncmp)),
        ],
        out_specs=pl.BlockSpec((None, NSA_QB, 512), lambda b, i: (b, i, 0)),
        out_shape=jax.ShapeDtypeStruct((B, S, 512), _ACT),
        scratch_shapes=[pltpu.VMEM((16 + ncmp, 512), F32), pltpu.VMEM((2, nblk, NSA_QB), F32),
                        pltpu.VMEM((2, 8 + nblk, NSA_QB), F32), pltpu.VMEM((2, 8 + nblk, NSA_QB), F32)],
        compiler_params=_cp("parallel", "arbitrary"),
        name="nsa_attention",
    )(z, z, kcmp, vcmp, kv4, kv4, kv4, kv4, cmpa, t_cmp, t_near, t_win, ovl)


CONV_T = 256
CONV_SUB = 64
CONV_HALO = 32


def _conv_kernel(z_ref, w_ref, b_ref, g_ref, beta_ref, o_ref, ubuf):
    t = pl.program_id(1)

    @pl.when(t == 0)
    def _():
        ubuf[0:CONV_HALO, :] = jnp.zeros((CONV_HALO, CONV_WIDTH), F32)

    zt = z_ref[...].astype(F32)
    ubuf[CONV_HALO:CONV_HALO + CONV_T, :] = zt[:, :CONV_WIDTH] * _sigmoid(zt[:, CONV_WIDTH:])
    for r in range(0, CONV_T, CONV_SUB):
        acc = jnp.zeros((CONV_SUB, CONV_WIDTH), F32)
        for k in range(CONV_K):
            off = CONV_HALO + r - (CONV_K - 1) + k
            acc = acc + w_ref[k:k + 1, :] * ubuf[off:off + CONV_SUB, :]
        y = _layer_norm(acc + b_ref[...], g_ref[...], beta_ref[...])
        o_ref[r:r + CONV_SUB, :] = (y * _sigmoid(y)).astype(o_ref.dtype)
    ubuf[0:CONV_HALO, :] = ubuf[CONV_T:CONV_T + CONV_HALO, :]


def _conformer_conv(z, conv_w, conv_b, conv_g, conv_beta):
    B, S, _ = z.shape
    w = jnp.concatenate([conv_w, jnp.zeros((1, CONV_WIDTH), conv_w.dtype)], axis=0).astype(F32)
    row = lambda a: a.reshape(1, CONV_WIDTH).astype(F32)
    const = lambda shape: pl.BlockSpec(shape, lambda b, t: (0, 0))
    return pl.pallas_call(
        _conv_kernel,
        grid=(B, S // CONV_T),
        in_specs=[pl.BlockSpec((None, CONV_T, 1024), lambda b, t: (b, t, Z_CONV // 1024)),
                  const((CONV_K + 1, CONV_WIDTH)), const((1, CONV_WIDTH)), const((1, CONV_WIDTH)),
                  const((1, CONV_WIDTH))],
        out_specs=pl.BlockSpec((None, CONV_T, CONV_WIDTH), lambda b, t: (b, t, 0)),
        out_shape=jax.ShapeDtypeStruct((B, S, CONV_WIDTH), _ACT),
        scratch_shapes=[pltpu.VMEM((CONV_HALO + CONV_T, CONV_WIDTH), F32)],
        compiler_params=_cp("parallel", "arbitrary"),
        name="conformer_conv",
    )(z, w, row(conv_b), row(conv_g), row(conv_beta))


GLA_T = 256
_GLA_LEVELS = (1, 2, 4, 8, 16, 32)


def _gla_masks():
    t = np.arange(GLA_CHUNK)[:, None]
    s = np.arange(GLA_CHUNK)[None, :]
    ms = [(t == s)]
    for c in _GLA_LEVELS:
        ms.append(((t // c) % 2 == 1) & (s // c == t // c - 1))
    m = np.stack(ms).astype(np.float32)
    return np.tile(m, (1, GLA_HEADS, 1))


def _gla_kernel(gq_ref, gk_ref, gv_ref, sm_ref, gr_ref, gw_ref, gb_ref, gg_ref, tril_ref, lm_ref, hm_ref,
                o_ref, state):
    @pl.when(pl.program_id(1) == 0)
    def _():
        state[...] = jnp.zeros(state.shape, F32)

    C = GLA_CHUNK
    tidx = lax.broadcasted_iota(jnp.int32, (C, GLA_HEADS * GLA_DK), 0)

    def stack_heads(x):
        return jnp.concatenate([x * hm_ref[h:h + 1, :] for h in range(GLA_HEADS)], axis=0)

    def chunk(ci, carry):
        r0 = pl.multiple_of(ci * C, C)
        q = gq_ref[pl.ds(r0, C), :].astype(F32) * (GLA_DK ** -0.5)
        k = gk_ref[pl.ds(r0, C), :].astype(F32)
        v = gv_ref[pl.ds(r0, C), :]
        pre = _dot(sm_ref[pl.ds(r0, C), :], gw_ref[...]) + gb_ref[...]
        la = (jnp.minimum(pre, 0.0) - jnp.log1p(jnp.exp(-jnp.abs(pre)))) * (1.0 / GLA_TAU)
        b = _dot01_left(tril_ref[...], la)
        attn = _dot_nt(stack_heads(q), k) * lm_ref[0]
        bstart = b
        bnext = pltpu.roll(b, C - 1, axis=0)
        for li, c in enumerate(_GLA_LEVELS):
            odd = (tidx // c) % 2 == 1
            q_l = jnp.where(odd, q * jnp.exp(b - bstart), 0.0)
            k_l = jnp.where(odd, 0.0, k * jnp.exp(bnext - b))
            attn = attn + _dot_nt(stack_heads(q_l), k_l) * lm_ref[li + 1]
            half = (tidx % (2 * c)) < c
            bstart = jnp.where(half, bstart, pltpu.roll(bstart, c, axis=0))
            bnext = jnp.where(half, pltpu.roll(bnext, C - c, axis=0), bnext)
        st = state[...]
        r_intra = _dot(attn, v)
        r_inter = _dot_nt(stack_heads(q * jnp.exp(b)), st)
        b_last = b[C - 1:C, :]
        upd = _dot_tn(v, k * jnp.exp(b_last - b))
        new_st = st * jnp.exp(b_last) + jnp.concatenate(
            [upd[128 * h:128 * h + 128, 64 * h:64 * h + 64] for h in range(GLA_HEADS)], axis=1)
        state[...] = new_st
        outs = []
        for h in range(GLA_HEADS):
            o = r_intra[64 * h:64 * h + 64, 128 * h:128 * h + 128] + r_inter[64 * h:64 * h + 64, :]
            ms = jnp.mean(o * o, axis=-1, keepdims=True)
            outs.append(o * lax.rsqrt(ms + 1e-6) * gg_ref[...])
        gr = gr_ref[pl.ds(r0, C), :].astype(F32)
        o_ref[pl.ds(r0, C), :] = (jnp.concatenate(outs, axis=1) * (gr * _sigmoid(gr))).astype(o_ref.dtype)
        return carry

    lax.fori_loop(0, GLA_T // C, chunk, 0)


def _gla(z, gla_w, gla_b, gla_g):
    B, S, _ = z.shape
    gw = jnp.zeros((128, GLA_HEADS * GLA_DK), F32).at[SMALL_GA:SMALL_GA + GLA_RANK].set(gla_w).astype(_MM)
    tril = jnp.asarray(np.tril(np.ones((GLA_CHUNK, GLA_CHUNK), np.float32)), dtype=_MM)
    lm = jnp.asarray(_gla_masks())
    hm = jnp.asarray(np.repeat(np.eye(GLA_HEADS, dtype=np.float32), GLA_DK, axis=1))
    const = lambda shape: pl.BlockSpec(shape, lambda b, t: (0,) * len(shape))
    zspec = lambda w, off: pl.BlockSpec((None, GLA_T, w), lambda b, t: (b, t, off // w))
    return pl.pallas_call(
        _gla_kernel,
        grid=(B, S // GLA_T),
        in_specs=[zspec(256, Z_GQ), zspec(256, Z_GK), zspec(512, Z_GV), zspec(128, Z_SMALL), zspec(512, Z_GR),
                  const((128, 256)), const((1, 256)), const((1, GLA_DV)), const((GLA_CHUNK, GLA_CHUNK)),
                  const((7, GLA_HEADS * GLA_CHUNK, GLA_CHUNK)), const((GLA_HEADS, 256))],
        out_specs=pl.BlockSpec((None, GLA_T, 512), lambda b, t: (b, t, 0)),
        out_shape=jax.ShapeDtypeStruct((B, S, 512), _ACT),
        scratch_shapes=[pltpu.VMEM((GLA_DV, GLA_HEADS * GLA_DK), F32)],
        compiler_params=_cp("parallel", "arbitrary"),
        name="gla",
    )(z, z, z, z, z, gw, gla_b.reshape(1, -1).astype(F32), gla_g.reshape(1, -1).astype(F32), tril, lm, hm)


MERGE_T = 512


def _merge_kernel(oa_ref, ob_ref, oc_ref, mg_ref, wb_ref, wo_ref, x_ref, g_ref, b_ref, o_ref):
    merged = None
    for j, r in enumerate((oa_ref, ob_ref, oc_ref)):
        gate = _sigmoid(mg_ref[:, D_MODEL * j:D_MODEL * (j + 1)].astype(F32))
        term = gate * _dot(r[...], wb_ref[j])
        merged = term if merged is None else merged + term
    mix = _dot(merged, wo_ref[...])
    o_ref[...] = _layer_norm(DEEPNORM_ALPHA * x_ref[...] + mix, g_ref[...], b_ref[...])


def _merge(oa, ob, oc, z2, w_branch, w_out, x2, g, b):
    n = x2.shape[0]
    tok = lambda w, blk=0: pl.BlockSpec((MERGE_T, w), lambda i: (i, blk))
    const = lambda shape: pl.BlockSpec(shape, lambda i: (0,) * len(shape))
    return pl.pallas_call(
        _merge_kernel,
        grid=(n // MERGE_T,),
        in_specs=[tok(512), tok(512), tok(512), tok(3072, Z_MERGE // 3072), const((3, 512, D_MODEL)),
                  const((D_MODEL, D_MODEL)), tok(D_MODEL), const((1, D_MODEL)), const((1, D_MODEL))],
        out_specs=tok(D_MODEL),
        out_shape=jax.ShapeDtypeStruct((n, D_MODEL), F32),
        compiler_params=_cp("parallel"),
        name="merge_outproj_ln",
    )(oa, ob, oc, z2, w_branch.astype(_MM), w_out.astype(_MM), x2, g.reshape(1, -1), b.reshape(1, -1))


XA_T = 512


def _xattn_kernel(x_ref, kv_ref, wq_ref, wo_ref, g_ref, b_ref, o_ref, o3_ref):
    x = x_ref[...]
    q = (_dot(x, wq_ref[...]) * (XA_DH ** -0.5)).astype(_MM)
    heads = []
    for h in range(XA_HEADS):
        hs = slice(XA_DH * h, XA_DH * (h + 1))
        s = _dot_nt(q[:, hs], kv_ref[:, hs])
        e = jnp.exp(s - jnp.max(s, axis=-1, keepdims=True))
        p = e / jnp.sum(e, axis=-1, keepdims=True)
        heads.append(_dot(p, kv_ref[:, D_MODEL + XA_DH * h:D_MODEL + XA_DH * (h + 1)]))
    xa = _dot(jnp.concatenate(heads, axis=1), wo_ref[...])
    y = _layer_norm(DEEPNORM_ALPHA * x + xa, g_ref[...], b_ref[...])
    o_ref[...] = y
    o3_ref[:, 0, :] = y


def _cross_attention(x, kv, wq, wo, g, b):
    B, S, _ = x.shape
    M = kv.shape[1]
    nt = S // XA_T
    const = lambda shape: pl.BlockSpec(shape, lambda bb, t: (0,) * len(shape))
    return pl.pallas_call(
        _xattn_kernel,
        grid=(B, nt),
        in_specs=[pl.BlockSpec((None, XA_T, D_MODEL), lambda bb, t: (bb, t, 0)),
                  pl.BlockSpec((None, M, 2 * D_MODEL), lambda bb, t: (bb, 0, 0)),
                  const((D_MODEL, D_MODEL)), const((D_MODEL, D_MODEL)), const((1, D_MODEL)), const((1, D_MODEL))],
        out_specs=[pl.BlockSpec((None, XA_T, D_MODEL), lambda bb, t: (bb, t, 0)),
                   pl.BlockSpec((XA_T, 1, D_MODEL), lambda bb, t: (bb * nt + t, 0, 0))],
        out_shape=[jax.ShapeDtypeStruct((B, S, D_MODEL), F32), jax.ShapeDtypeStruct((B * S, 1, D_MODEL), F32)],
        compiler_params=_cp("parallel", "parallel"),
        name="cross_attention_ln",
    )(x, kv, wq.astype(_MM), wo.astype(_MM), g.reshape(1, -1), b.reshape(1, -1))


ROUTE_T = 512


def _router_kernel(x_ref, wt_ref, b_ref, upper_ref, ti_ref, gate_ref, rank_ref, cnt_ref, carry):
    @pl.when(pl.program_id(0) == 0)
    def _():
        carry[...] = jnp.zeros(carry.shape, F32)

    x = x_ref[...]
    xh = x.astype(_MM)
    xl = (x - xh.astype(F32)).astype(_MM)
    w = wt_ref[...]
    wh = w.astype(_MM)
    wl = (w - wh.astype(F32)).astype(_MM)
    logits = _dot_nt(wh, xh) + _dot_nt(wh, xl) + _dot_nt(wl, xh) + b_ref[...]
    eidx = lax.broadcasted_iota(jnp.int32, logits.shape, 0)
    v = logits
    tops, idxs, hots = [], [], []
    for _ in range(TOP_K):
        m = jnp.max(v, axis=0, keepdims=True)
        idx = jnp.min(jnp.where(v == m, eidx, N_EXPERTS), axis=0, keepdims=True)
        hot = eidx == idx
        v = jnp.where(hot, -jnp.inf, v)
        tops.append(m)
        idxs.append(idx)
        hots.append(jnp.where(hot, 1.0, 0.0))
    es = [jnp.exp(t - tops[0]) for t in tops]
    den = es[0] + es[1] + es[2] + es[3]
    multihot = hots[0] + hots[1] + hots[2] + hots[3]
    before = jnp.dot(multihot.astype(_MM), upper_ref[...], preferred_element_type=F32) + carry[...][:, 0:1]
    ranks = [jnp.sum(h * before, axis=0, keepdims=True) for h in hots]
    pad = jnp.zeros((8 - TOP_K, x.shape[0]), F32)
    ti_ref[...] = jnp.concatenate(idxs + [pad.astype(jnp.int32)], axis=0)
    gate_ref[...] = jnp.concatenate([e / den for e in es] + [pad], axis=0)
    rank_ref[...] = jnp.concatenate(ranks + [pad], axis=0).astype(jnp.int32)
    carry[...] = carry[...] + jnp.sum(multihot, axis=1, keepdims=True)
    cnt_ref[...] = carry[...]


def _router(x2, router_w, router_b):
    n = x2.shape[0]
    upper = jnp.asarray(np.triu(np.ones((ROUTE_T, ROUTE_T), np.float32), 1), dtype=_MM)
    bcol = jnp.broadcast_to(router_b.astype(F32)[:, None], (N_EXPERTS, ROUTE_T))
    tokspec = pl.BlockSpec((8, ROUTE_T), lambda i: (0, i))
    const = lambda shape: pl.BlockSpec(shape, lambda i: (0,) * len(shape))
    return pl.pallas_call(
        _router_kernel,
        grid=(n // ROUTE_T,),
        in_specs=[pl.BlockSpec((ROUTE_T, D_MODEL), lambda i: (i, 0)), const((N_EXPERTS, D_MODEL)),
                  const((N_EXPERTS, ROUTE_T)), const((ROUTE_T, ROUTE_T))],
        out_specs=[tokspec, tokspec, tokspec, const((N_EXPERTS, 128))],
        out_shape=[jax.ShapeDtypeStruct((8, n), jnp.int32), jax.ShapeDtypeStruct((8, n), F32),
                   jax.ShapeDtypeStruct((8, n), jnp.int32), jax.ShapeDtypeStruct((N_EXPERTS, 128), F32)],
        scratch_shapes=[pltpu.VMEM((N_EXPERTS, 128), F32)],
        compiler_params=_cp("arbitrary"),
        name="moe_router",
    )(x2, router_w.T.astype(F32), bcol, upper)


def _moe_kernel(blk_e_ref, nvalid_ref, tokc_ref, tokn_ref, dstp_ref, x_hbm, wgu_ref, bgu_ref, wdn_ref, bdn_ref,
                y_hbm, xbuf, ybuf, wgu_mm, wdn_mm, sem_in, sem_out):
    b = pl.program_id(0)
    nvalid = nvalid_ref[0]
    slot = b % 2
    other = 1 - slot

    def gather_row(tok_ref, s, r, priority=0):
        pltpu.make_async_copy(x_hbm.at[tok_ref[0, r]], xbuf.at[s, pl.ds(r, 1), :], sem_in.at[s]).start(priority)

    def scatter_row(s, r, priority=0):
        pltpu.make_async_copy(ybuf.at[s, pl.ds(r, 1), :], y_hbm.at[dstp_ref[0, r]], sem_out.at[s]).start(priority)

    def wait_in(s):
        pltpu.make_async_copy(x_hbm.at[pl.ds(0, MOE_BLK), 0], xbuf.at[s], sem_in.at[s]).wait()

    def wait_out(s):
        pltpu.make_async_copy(ybuf.at[s], y_hbm.at[pl.ds(0, MOE_BLK), 0], sem_out.at[s]).wait()

    @pl.when(b == 0)
    def _():
        ybuf[...] = jnp.zeros(ybuf.shape, F32)

        def body(r, c):
            gather_row(tokc_ref, 0, r)
            return c
        lax.fori_loop(0, MOE_BLK, body, 0)

    @pl.when(jnp.logical_and(b >= 1, b <= nvalid))
    def _():
        wait_out(slot)

    @pl.when(jnp.logical_and(b < nvalid, jnp.logical_or(b == 0, blk_e_ref[b] != blk_e_ref[jnp.maximum(b - 1, 0)])))
    def _():
        wgu_mm[...] = wgu_ref[...].astype(_MM)
        wdn_mm[...] = wdn_ref[...].astype(_MM)

    @pl.when(b < nvalid)
    def _():
        wait_in(slot)
        xb = xbuf[slot].astype(_MM)
        for r in range(MOE_BLK):
            gather_row(tokn_ref, other, r, r % 2)
        gu = jnp.dot(xb, wgu_mm[...], preferred_element_type=F32) + bgu_ref[...]
        for r in range(MOE_BLK):
            scatter_row(other, r, r % 2)
        g = jnp.minimum(gu[:, :D_MODEL], SWIGLU_LIMIT)
        u = jnp.clip(gu[:, D_MODEL:], -SWIGLU_LIMIT, SWIGLU_LIMIT)
        h = (u + 1.0) * g * _sigmoid(SWIGLU_ALPHA * g)
        ybuf[slot] = jnp.dot(h.astype(_MM), wdn_mm[...], preferred_element_type=F32) + bdn_ref[...]

    @pl.when(b == nvalid)
    def _():
        def body(r, c):
            scatter_row(other, r)
            return c
        lax.fori_loop(0, MOE_BLK, body, 0)
        wait_out(other)
        wait_in(slot)


def _moe_experts(x2, slot_tok, slot_dst, blk_e, nvalid, n_out_rows, w_gu, b_gu, w_dn, b_dn):
    n_blocks = blk_e.shape[0]
    tok3 = slot_tok.reshape(n_blocks, 1, MOE_BLK)
    dst3 = slot_dst.reshape(n_blocks + 1, 1, MOE_BLK)
    smem = lambda f: pl.BlockSpec((None, 1, MOE_BLK), f, memory_space=pltpu.SMEM)
    grid_spec = pltpu.PrefetchScalarGridSpec(
        num_scalar_prefetch=2,
        grid=(n_blocks,),
        in_specs=[
            smem(lambda b, be, nv: (b, 0, 0)),
            smem(lambda b, be, nv: (jnp.minimum(b + 1, n_blocks - 1), 0, 0)),
            smem(lambda b, be, nv: (b, 0, 0)),
            pl.BlockSpec(memory_space=pl.ANY),
            pl.BlockSpec((None, D_MODEL, 2 * D_MODEL), lambda b, be, nv: (be[b], 0, 0)),
            pl.BlockSpec((None, 1, 2 * D_MODEL), lambda b, be, nv: (be[b], 0, 0)),
            pl.BlockSpec((None, D_MODEL, D_MODEL), lambda b, be, nv: (be[b], 0, 0)),
            pl.BlockSpec((None, 1, D_MODEL), lambda b, be, nv: (be[b], 0, 0)),
        ],
        out_specs=pl.BlockSpec(memory_space=pl.ANY),
        scratch_shapes=[pltpu.VMEM((2, MOE_BLK, D_MODEL), F32), pltpu.VMEM((2, MOE_BLK, D_MODEL), F32),
                        pltpu.VMEM((D_MODEL, 2 * D_MODEL), _MM), pltpu.VMEM((D_MODEL, D_MODEL), _MM),
                        pltpu.SemaphoreType.DMA((2,)), pltpu.SemaphoreType.DMA((2,))],
    )
    return pl.pallas_call(
        _moe_kernel,
        grid_spec=grid_spec,
        out_shape=jax.ShapeDtypeStruct((n_out_rows, 1, D_MODEL), F32),
        compiler_params=_cp("arbitrary"),
        name="moe_experts",
    )(blk_e, nvalid, tok3, tok3, dst3, x2, w_gu, b_gu.reshape(N_EXPERTS, 1, -1), w_dn,
      b_dn.reshape(N_EXPERTS, 1, -1))


def _combine_kernel(y0_ref, y1_ref, y2_ref, y3_ref, gate_ref, x_ref, g_ref, b_ref, o_ref):
    gate = gate_ref[...]
    ff = gate[:, 0:1] * y0_ref[:, 0, :]
    for k, r in enumerate((y1_ref, y2_ref, y3_ref)):
        ff = ff + gate[:, k + 1:k + 2] * r[:, 0, :]
    o_ref[...] = _layer_norm(DEEPNORM_ALPHA * x_ref[...] + ff, g_ref[...], b_ref[...])


def _moe_combine(ys, gate_t, x2, g, b):
    n = x2.shape[0]
    steps = n // COMB_T
    const = lambda shape: pl.BlockSpec(shape, lambda i: (0,) * len(shape))
    yspec = lambda k: pl.BlockSpec((COMB_T, 1, D_MODEL), lambda i: (k * steps + i, 0, 0))
    return pl.pallas_call(
        _combine_kernel,
        grid=(steps,),
        in_specs=[yspec(0), yspec(1), yspec(2), yspec(3), pl.BlockSpec((COMB_T, 8), lambda i: (i, 0)),
                  pl.BlockSpec((COMB_T, D_MODEL), lambda i: (i, 0)), const((1, D_MODEL)), const((1, D_MODEL))],
        out_specs=pl.BlockSpec((COMB_T, D_MODEL), lambda i: (i, 0)),
        out_shape=jax.ShapeDtypeStruct((n, D_MODEL), F32),
        compiler_params=_cp("parallel"),
        name="moe_combine_ln",
    )(ys, ys, ys, ys, gate_t, x2, g.reshape(1, -1), b.reshape(1, -1))


def _moe_ffn(x2, x3, router_w, router_b, w_gu, b_gu, w_dn, b_dn, g, b):
    n = x2.shape[0]
    a = n * TOP_K
    top_i, gate, rank, cnt = _router(x2, router_w, router_b)
    counts = cnt[:, 0].astype(jnp.int32)
    padded = (counts + MOE_BLK - 1) // MOE_BLK * MOE_BLK
    pends = jnp.cumsum(padded)
    pstarts = pends - padded
    n_blocks = a // MOE_BLK + N_EXPERTS
    n_slots = n_blocks * MOE_BLK
    eids = jnp.arange(N_EXPERTS, dtype=jnp.int32)
    hot = top_i[:TOP_K, None, :] == eids[None, :, None]
    dest = jnp.sum(jnp.where(hot, pstarts[None, :, None], 0), axis=1) + rank[:TOP_K]
    spare = a + jnp.arange(n_slots, dtype=jnp.int32) % MOE_BLK
    slot_asg = spare.at[dest.reshape(-1)].set(jnp.arange(a, dtype=jnp.int32))
    slot_tok = jnp.where(slot_asg < a, slot_asg % n, 0)
    slot_dst = jnp.concatenate([spare[:MOE_BLK], slot_asg])
    blk_start = jnp.arange(n_blocks, dtype=jnp.int32) * MOE_BLK
    blk_e = jnp.minimum(jnp.sum((pends[None, :] <= blk_start[:, None]).astype(jnp.int32), axis=1), N_EXPERTS - 1)
    nvalid = (pends[-1:] // MOE_BLK).astype(jnp.int32)
    ys = _moe_experts(x3, slot_tok, slot_dst, blk_e, nvalid, a + MOE_BLK, w_gu, b_gu, w_dn, b_dn)
    return _moe_combine(ys, gate.T, x2, g, b)


def _layer(x, mem, tables, w_in, cmp_pe, cmp_w1, cmp_b1, cmp_w2, conv_w, conv_b, conv_g, conv_beta, gla_w, gla_b,
           gla_g, w_branch, w_out, xa_wq, xa_wkv, xa_wo, router_w, router_b, w_gu, b_gu, w_dn, b_dn, ng, nb):
    B, S, D = x.shape
    n = B * S
    x2 = x.reshape(n, D)
    z2 = _matmul(x2, _reorder_w_in(w_in), 1024, ZW // 5, _ACT)
    z = z2.reshape(B, S, ZW)
    kvc = z[:, :, Z_KV:Z_KV + 256].reshape(B, S // 16, 16, 2, 128).transpose(3, 0, 1, 2, 4)
    cmp = _compress(kvc.reshape(2, B, S // 16, 2048), cmp_pe, cmp_w1, cmp_b1, cmp_w2)
    kv4 = jnp.pad(z[:, :, Z_KV + 256:Z_KV + 768], ((0, 0), (WINDOW, 0), (0, 0)))
    oa = _nsa_attention(z, cmp[0], cmp[1], kv4, tables)
    ob = _conformer_conv(z, conv_w, conv_b, conv_g, conv_beta)
    oc = _gla(z, gla_w, gla_b, gla_g)
    x2 = _merge(oa.reshape(n, 512), ob.reshape(n, 512), oc.reshape(n, 512), z2, w_branch, w_out, x2, ng[0], nb[0])
    kv = _matmul(mem.reshape(-1, D), xa_wkv.astype(_MM), 1024, 1024, _ACT).reshape(B, -1, 2 * D)
    x2, x3 = _cross_attention(x2.reshape(B, S, D), kv, xa_wq, xa_wo, ng[1], nb[1])
    x2 = _moe_ffn(x2.reshape(n, D), x3, router_w, router_b, w_gu, b_gu, w_dn, b_dn, ng[2], nb[2])
    return x2.reshape(B, S, D)


def kernel(x, mem, rel_bias, w_in, cmp_pe, cmp_w1, cmp_b1, cmp_w2, conv_w, conv_b, conv_norm_g, conv_norm_b,
           gla_gate_w, gla_gate_b, gla_norm_g, w_branch, w_out, xa_wq, xa_wkv, xa_wo, router_w, router_b,
           expert_w_gu, expert_b_gu, expert_w_down, expert_b_down, norm_g, norm_b):
    tables = _nsa_tables(rel_bias)
    for l in range(DEPTH):
        x = _layer(x, mem, tables, w_in[l], cmp_pe[l], cmp_w1[l], cmp_b1[l], cmp_w2[l], conv_w[l], conv_b[l],
                   conv_norm_g[l], conv_norm_b[l], gla_gate_w[l], gla_gate_b[l], gla_norm_g[l], w_branch[l],
                   w_out[l], xa_wq[l], xa_wkv[l], xa_wo[l], router_w[l], router_b[l], expert_w_gu[l],
                   expert_b_gu[l], expert_w_down[l], expert_b_down[l], norm_g[l], norm_b[l])
    return x
```

```python
import functools
import math

import numpy as np
import jax
import jax.numpy as jnp
from jax import lax
from jax.experimental import pallas as pl
from jax.experimental.pallas import tpu as pltpu

F32 = jnp.float32
_MM = jnp.bfloat16
_ACT = jnp.bfloat16
NEG = -1e30
LOG2E = 1.4426950408889634

D_MODEL = 1024
DEPTH = 2
MEM_LEN = 256
NSA_HEADS = 8
NSA_DH = 64
CMP_LEN = 32
CMP_STRIDE = 16
SEL_BLOCK = 64
SEL_TOP = 16
WINDOW = 512
CONV_WIDTH = 512
CONV_K = 31
GLA_HEADS = 4
GLA_DK = 64
GLA_DV = 128
GLA_RANK = 16
GLA_TAU = 16.0
GLA_CHUNK = 64
REL_BUCKETS = 32
REL_MAX_DIST = 128
XA_HEADS = 4
XA_DH = 256
N_EXPERTS = 32
TOP_K = 4
SWIGLU_ALPHA = 1.702
SWIGLU_LIMIT = 7.0
DEEPNORM_ALPHA = (2 * DEPTH) ** 0.25

Z_MERGE = 0
Z_CONV = 3072
Z_Q = 4096
Z_GV = 4608
Z_GR = 5120
Z_KV = 5632
Z_GQ = 6400
Z_GK = 6656
Z_SMALL = 6912
ZW = 7040
SMALL_GA = 24

_IN_SIZES = (512, 768, 24, 1024, 256, 256, 512, 16, 512, 3072)
_IN_OFF = np.concatenate([[0], np.cumsum(_IN_SIZES)]).astype(int)

VMEM_LIMIT = 56 * 1024 * 1024

NSA_QB = 128
MOE_BLK = 512
COMB_T = 256


def _cp(*sem):
    return pltpu.CompilerParams(dimension_semantics=sem, vmem_limit_bytes=VMEM_LIMIT)


def _dot(a, b):
    return jnp.dot(a.astype(_MM), b.astype(_MM), preferred_element_type=F32)


def _dot_nt(a, b):
    return lax.dot_general(a.astype(_MM), b.astype(_MM), (((1,), (1,)), ((), ())), preferred_element_type=F32)


def _dot_tn(a, b):
    return lax.dot_general(a.astype(_MM), b.astype(_MM), (((0,), (0,)), ((), ())), preferred_element_type=F32)


def _split3(x):
    x1 = x.astype(_MM)
    r1 = x - x1.astype(F32)
    x2 = r1.astype(_MM)
    x3 = (r1 - x2.astype(F32)).astype(_MM)
    return x1, x2, x3


def _dot01_left(m01, x):
    x1, x2, x3 = _split3(x)
    return (jnp.dot(m01, x1, preferred_element_type=F32) + jnp.dot(m01, x2, preferred_element_type=F32)
            + jnp.dot(m01, x3, preferred_element_type=F32))


def _layer_norm(y, g, b, eps=1e-5):
    mu = jnp.mean(y, axis=-1, keepdims=True)
    d = y - mu
    var = jnp.mean(d * d, axis=-1, keepdims=True)
    return d * lax.rsqrt(var + eps) * g + b


def _sigmoid(x):
    return 1.0 / (1.0 + jnp.exp(-x))


def _mm_kernel(x_ref, w_ref, o_ref):
    o_ref[...] = _dot(x_ref[...], w_ref[...]).astype(o_ref.dtype)


def _matmul(x, w, tm, tn, out_dtype):
    m, k = x.shape
    n = w.shape[1]
    tm = min(tm, m)
    return pl.pallas_call(
        _mm_kernel,
        grid=(m // tm, n // tn),
        in_specs=[pl.BlockSpec((tm, k), lambda i, j: (i, 0)), pl.BlockSpec((k, tn), lambda i, j: (0, j))],
        out_specs=pl.BlockSpec((tm, tn), lambda i, j: (i, j)),
        out_shape=jax.ShapeDtypeStruct((m, n), out_dtype),
        compiler_params=_cp("parallel", "parallel"),
        name="matmul",
    )(x, w)


def _reorder_w_in(w_in):
    def cols(i):
        return w_in[:, _IN_OFF[i]:_IN_OFF[i + 1]]
    small = jnp.concatenate([cols(2), cols(7), jnp.zeros((D_MODEL, 128 - 40), w_in.dtype)], axis=1)
    w = jnp.concatenate([cols(9), cols(3), cols(0), cols(6), cols(8), cols(1), cols(4), cols(5), small], axis=1)
    return w.astype(_MM)


def _cmp_kernel(x_ref, pea_ref, peb_ref, wa_ref, wb_ref, b1_ref, w2_ref, o_ref):
    x = x_ref[...].astype(F32)
    a = _dot(x + pea_ref[...], wa_ref[...])
    bm = _dot(x + peb_ref[...], wb_ref[...])
    n = bm.shape[0]
    bs = pltpu.roll(bm, n - 1, axis=0)
    h = jax.nn.gelu(a + bs + b1_ref[...], approximate=True)
    o_ref[...] = _dot(h, w2_ref[...]).astype(o_ref.dtype)


def _compress(kv2, cmp_pe, cmp_w1, cmp_b1, cmp_w2):
    _, B, R, _ = kv2.shape
    eye2 = jnp.eye(2, dtype=F32)

    def half(w1h):
        w = jnp.einsum('rlde,gh->rlgdhe', w1h, eye2)
        return w.reshape(2, 16 * 128, 128).astype(_MM)

    wa = half(cmp_w1[:, :16])
    wb = half(cmp_w1[:, 16:])

    def pe_half(p):
        return jnp.broadcast_to(p[:, :, None, :], (2, 16, 2, 64)).reshape(2, 1, 2048).astype(F32)

    pea = pe_half(cmp_pe[:, :16])
    peb = pe_half(cmp_pe[:, 16:])
    b1 = jnp.tile(cmp_b1, (1, 2)).reshape(2, 1, 128).astype(F32)
    w2 = jnp.einsum('rde,gh->rgdhe', cmp_w2, eye2).reshape(2, 128, 128).astype(_MM)
    return pl.pallas_call(
        _cmp_kernel,
        grid=(2, B),
        in_specs=[
            pl.BlockSpec((None, None, R, 2048), lambda r, b: (r, b, 0, 0)),
            pl.BlockSpec((None, 1, 2048), lambda r, b: (r, 0, 0)),
            pl.BlockSpec((None, 1, 2048), lambda r, b: (r, 0, 0)),
            pl.BlockSpec((None, 2048, 128), lambda r, b: (r, 0, 0)),
            pl.BlockSpec((None, 2048, 128), lambda r, b: (r, 0, 0)),
            pl.BlockSpec((None, 1, 128), lambda r, b: (r, 0, 0)),
            pl.BlockSpec((None, 128, 128), lambda r, b: (r, 0, 0)),
        ],
        out_specs=pl.BlockSpec((None, None, R, 128), lambda r, b: (r, b, 0, 0)),
        out_shape=jax.ShapeDtypeStruct((2, B, R, 128), _ACT),
        compiler_params=_cp("parallel", "parallel"),
        name="nsa_compress",
    )(kv2, pea, peb, wa, wb, b1, w2)


def _t5_bucket_np(dist):
    n = np.maximum(dist, 0)
    exact = REL_BUCKETS // 2
    lr = np.log(np.maximum(n, 1).astype(np.float32) / np.float32(exact)) / np.float32(math.log(REL_MAX_DIST / exact))
    large = exact + (lr * np.float32(REL_BUCKETS - exact)).astype(np.int32)
    return np.where(n < exact, n, np.minimum(large, REL_BUCKETS - 1))


def _nsa_tables(rel_bias):
    rel = (rel_bias - rel_bias[REL_BUCKETS - 1:REL_BUCKETS]).astype(F32) * LOG2E
    q = np.arange(NSA_QB)

    def table(dist, valid, fill):
        hot = jnp.asarray(_t5_bucket_np(dist), jnp.int32)[:, :, None] == jnp.arange(REL_BUCKETS)[None, None, :]
        t = jnp.sum(jnp.where(hot[..., None], rel[None, None], 0.0), axis=2)
        t = jnp.where(valid[:, :, None], t, fill)
        return jnp.transpose(t, (0, 2, 1)).reshape(dist.shape[0], NSA_HEADS * NSA_QB)

    cc = np.arange(24) - 16
    d_c = q[None, :] - CMP_STRIDE * cc[:, None] - (CMP_LEN - 1)
    t_cmp = table(d_c, d_c >= 0, 0.0)
    ko = np.arange(256)
    d_n = q[None, :] + 128 - ko[:, None]
    t_near = table(d_n, d_n >= 0, NEG)
    kw = np.arange(WINDOW + NSA_QB)
    d_w = q[None, :] + WINDOW - kw[:, None]
    in_win = (d_w >= 0) & (d_w < WINDOW)
    t_win = jnp.stack([table(d_w, in_win & (kw[:, None] >= WINDOW - NSA_QB * v), NEG) for v in range(5)])
    return t_cmp, t_near, t_win


def _nsa_kernel(q_ref, gate_ref, kcmp_ref, vcmp_ref, ksel_ref, vsel_ref, kwin_ref, vwin_ref,
                cmpa_ref, tcmp_ref, tnear_ref, twin_ref, ovl_ref, out_ref, s_scr, v_scr, sel_scr, far_scr):
    i = pl.program_id(1)
    ncmp = kcmp_ref.shape[0]
    nblk = ovl_ref.shape[0]
    qt = (q_ref[...].astype(F32) * (NSA_DH ** -0.5 * LOG2E)).T
    gt = _sigmoid(gate_ref[...].astype(F32)).T
    zeros64 = jnp.zeros((64, NSA_QB), F32)
    jidx = lax.broadcasted_iota(jnp.int32, (nblk, NSA_QB), 0)
    qidx = lax.broadcasted_iota(jnp.int32, (nblk, NSA_QB), 1)
    sub8 = lax.broadcasted_iota(jnp.int32, (8, NSA_QB), 0)
    cur = 2 * i + (qidx >= SEL_BLOCK).astype(jnp.int32)
    forced = (jidx == 0) | (jidx == cur) | (jidx == cur - 1)
    future = jidx > cur
    neg8 = jnp.full((8, NSA_QB), NEG, F32)
    s_scr[0:16, :] = jnp.zeros((16, 4 * NSA_QB), F32)

    def mask_rows(scr, first_block, n):
        rows = [jnp.broadcast_to(scr[pl.ds(8 + first_block + b, 1), :], (SEL_BLOCK, NSA_QB)) for b in range(n)]
        return jnp.concatenate(rows, axis=0)

    def add_mask(s, m):
        return jnp.concatenate([s[:, 128 * h:128 * h + 128] + m for h in range(4)], axis=1)

    q_gs, o_cs, o_ws, state = [], [], [], []
    for g in range(2):
        blocks = []
        for h in range(4):
            r = qt[64 * (4 * g + h):64 * (4 * g + h) + 64]
            blocks.append(jnp.concatenate([r, zeros64] if g == 0 else [zeros64, r], axis=0))
        q_g = jnp.concatenate(blocks, axis=1).astype(_MM)
        q_gs.append(q_g)
        cs = slice(512 * g, 512 * g + 512)

        s_scr[16:16 + ncmp, :] = _dot(kcmp_ref[...], q_g)
        w0 = pl.multiple_of(8 * i, 8)
        s_scr[pl.ds(w0, 24), :] = s_scr[pl.ds(w0, 24), :] + tcmp_ref[:, cs]
        sc = jnp.where(cmpa_ref[...] <= NSA_QB * i, s_scr[16:16 + ncmp, :], NEG)
        m = jnp.maximum(jnp.max(sc, axis=0, keepdims=True), 0.1 * NEG)
        e = jnp.exp2(sc - m)
        l = jnp.sum(e, axis=0, keepdims=True)
        pc = e / jnp.maximum(l, 1e-30)
        o_cs.append(_dot_tn(vcmp_ref[...], pc))

        psum = pc[:, 0:128] + pc[:, 128:256] + pc[:, 256:384] + pc[:, 384:512]
        imp = _dot01_left(ovl_ref[...], psum)
        v = jnp.where(forced, 1e30, jnp.where(future, -1.0, imp))
        v_scr[g] = v
        sel_scr[g, 0:8, :] = neg8
        far_scr[g, 0:8, :] = neg8
        for r8 in range(0, nblk, 8):
            vr = v[r8:r8 + 8]
            cnt = jnp.zeros((8, NSA_QB), F32)
            for jp in range(nblk):
                row = jnp.broadcast_to(v_scr[g, pl.ds(jp, 1), :], (8, NSA_QB))
                if jp < r8:
                    cnt = cnt + jnp.where(row >= vr, 1.0, 0.0)
                elif jp >= r8 + 8:
                    cnt = cnt + jnp.where(row > vr, 1.0, 0.0)
                else:
                    cnt = cnt + jnp.where(sub8 > jp - r8, jnp.where(row >= vr, 1.0, 0.0),
                                          jnp.where(row > vr, 1.0, 0.0))
            sel = jnp.where(cnt < float(min(SEL_TOP, nblk)), 0.0, NEG)
            sel_scr[g, 8 + r8:16 + r8, :] = sel
            far_scr[g, 8 + r8:16 + r8, :] = jnp.where(jidx[r8:r8 + 8] <= 2 * i - 3, sel, NEG)

        n0 = pl.multiple_of(NSA_QB * i + WINDOW - 128, 128)
        sn = add_mask(_dot(ksel_ref[pl.ds(n0, 256), :], q_g) + tnear_ref[:, cs], mask_rows(sel_scr.at[g], 2 * i - 2, 4))
        m_s = jnp.max(sn, axis=0, keepdims=True)
        e = jnp.exp2(sn - m_s)
        state += [m_s, jnp.sum(e, axis=0, keepdims=True), _dot_tn(vsel_ref[pl.ds(n0, 256), :], e)]

        w_start = pl.multiple_of(NSA_QB * i, 128)
        sw = _dot(kwin_ref[pl.ds(w_start, WINDOW + NSA_QB), :], q_g) + twin_ref[:, cs]
        m_w = jnp.max(sw, axis=0, keepdims=True)
        e = jnp.exp2(sw - m_w)
        l_w = jnp.sum(e, axis=0, keepdims=True)
        o_ws.append(_dot_tn(vwin_ref[pl.ds(w_start, WINDOW + NSA_QB), :], e) / l_w)

    def far_body(c, carry):
        k0 = pl.multiple_of(WINDOW + 512 * c, 512)
        kf = ksel_ref[pl.ds(k0, 512), :]
        vf = vsel_ref[pl.ds(k0, 512), :]
        new = []
        for g in range(2):
            m_o, l_o, a_o = carry[3 * g:3 * g + 3]
            sf = add_mask(_dot(kf, q_gs[g]), mask_rows(far_scr.at[g], 8 * c, 8))
            m_n = jnp.maximum(m_o, jnp.max(sf, axis=0, keepdims=True))
            alpha = jnp.exp2(m_o - m_n)
            ef = jnp.exp2(sf - m_n)
            new += [m_n, alpha * l_o + jnp.sum(ef, axis=0, keepdims=True), alpha * a_o + _dot_tn(vf, ef)]
        return tuple(new)

    state = lax.fori_loop(0, (2 * i + 5) // 8, far_body, tuple(state))

    out_rows = []
    for g in range(2):
        o_s = state[3 * g + 2] / state[3 * g + 1]
        for h in range(4):
            hs = slice(128 * h, 128 * h + 128)
            ds_ = slice(64 * g, 64 * g + 64)
            gi = (4 * g + h) * 3
            out_rows.append(gt[gi:gi + 1, :] * o_cs[g][ds_, hs] + gt[gi + 1:gi + 2, :] * o_s[ds_, hs]
                            + gt[gi + 2:gi + 3, :] * o_ws[g][ds_, hs])
    out_ref[...] = jnp.concatenate(out_rows, axis=0).T.astype(out_ref.dtype)


def _nsa_attention(z, kcmp, vcmp, kv4, tables):
    B, S, _ = z.shape
    n_steps = S // NSA_QB
    nblk = S // SEL_BLOCK
    ncmp = kcmp.shape[1]
    t_cmp, t_near, t_win = tables
    cmpa = (CMP_STRIDE * np.arange(ncmp)[:, None] + (CMP_LEN - 1) - np.arange(NSA_QB)[None, :]).astype(np.int32)
    cmpa = jnp.asarray(np.tile(cmpa, (1, 4)))
    c = np.arange(ncmp)[None, :]
    j = np.arange(nblk)[:, None]
    ovl = ((CMP_STRIDE * c < (j + 1) * SEL_BLOCK) & (CMP_STRIDE * c + CMP_LEN > j * SEL_BLOCK)
           & (c < (S - CMP_LEN) // CMP_STRIDE + 1))
    ovl = jnp.asarray(ovl.astype(np.float32), dtype=_MM)
    sp = S + WINDOW
    const = lambda shape: pl.BlockSpec(shape, lambda b, i: (0,) * len(shape))
    return pl.pallas_call(
        _nsa_kernel,
        grid=(B, n_steps),
        in_specs=[
            pl.BlockSpec((None, NSA_QB, 512), lambda b, i: (b, i, Z_Q // 512)),
            pl.BlockSpec((None, NSA_QB, 128), lambda b, i: (b, i, Z_SMALL // 128)),
            pl.BlockSpec((None, ncmp, 128), lambda b, i: (b, 0, 0)),
            pl.BlockSpec((None, ncmp, 128), lambda b, i: (b, 0, 0)),
            pl.BlockSpec((None, sp, 128), lambda b, i: (b, 0, 0)),
            pl.BlockSpec((None, sp, 128), lambda b, i: (b, 0, 1)),
            pl.BlockSpec((None, sp, 128), lambda b, i: (b, 0, 2)),
            pl.BlockSpec((None, sp, 128), lambda b, i: (b, 0, 3)),
            const((ncmp, 512)), const((24, 1024)), const((256, 1024)),
            pl.BlockSpec((None, WINDOW + NSA_QB, 1024), lambda b, i: (jnp.minimum(i, 4), 0, 0)),
            const((nblk, ncmp)),
        ],
        out_specs=pl.BlockSpec((None, NSA_QB, 512), lambda b, i: (b, i, 0)),
        out_shape=jax.ShapeDtypeStruct((B, S, 512), _ACT),
        scratch_shapes=[pltpu.VMEM((16 + ncmp, 512), F32), pltpu.VMEM((2, nblk, NSA_QB), F32),
                        pltpu.VMEM((2, 8 + nblk, NSA_QB), F32), pltpu.VMEM((2, 8 + nblk, NSA_QB), F32)],
        compiler_params=_cp("parallel", "arbitrary"),
        name="nsa_attention",
    )(z, z, kcmp, vcmp, kv4, kv4, kv4, kv4, cmpa, t_cmp, t_near, t_win, ovl)


CONV_T = 256
CONV_SUB = 64
CONV_HALO = 32


def _conv_kernel(z_ref, w_ref, b_ref, g_ref, beta_ref, o_ref, ubuf):
    t = pl.program_id(1)

    @pl.when(t == 0)
    def _():
        ubuf[0:CONV_HALO, :] = jnp.zeros((CONV_HALO, CONV_WIDTH), F32)
        ubuf[CONV_HALO + CONV_T:, :] = jnp.zeros((8, CONV_WIDTH), F32)

    zt = z_ref[...].astype(F32)
    ubuf[CONV_HALO:CONV_HALO + CONV_T, :] = zt[:, :CONV_WIDTH] * _sigmoid(zt[:, CONV_WIDTH:])
    for r in range(0, CONV_T, CONV_SUB):
        acc = None
        for s in range(8):
            p = None
            for a in range((CONV_K + 1) // 8 + 1):
                k = 8 * a + s - (CONV_HALO - CONV_K + 1)
                if 0 <= k < CONV_K:
                    term = w_ref[k:k + 1, :] * ubuf[r + 8 * a:r + 8 * a + CONV_SUB + 8, :]
                    p = term if p is None else p + term
            acc = p[s:s + CONV_SUB] if acc is None else acc + p[s:s + CONV_SUB]
        y = _layer_norm(acc + b_ref[...], g_ref[...], beta_ref[...])
        o_ref[r:r + CONV_SUB, :] = (y * _sigmoid(y)).astype(o_ref.dtype)
    ubuf[0:CONV_HALO, :] = ubuf[CONV_T:CONV_T + CONV_HALO, :]


def _conformer_conv(z, conv_w, conv_b, conv_g, conv_beta):
    B, S, _ = z.shape
    w = jnp.concatenate([conv_w, jnp.zeros((1, CONV_WIDTH), conv_w.dtype)], axis=0).astype(F32)
    row = lambda a: a.reshape(1, CONV_WIDTH).astype(F32)
    const = lambda shape: pl.BlockSpec(shape, lambda b, t: (0, 0))
    return pl.pallas_call(
        _conv_kernel,
        grid=(B, S // CONV_T),
        in_specs=[pl.BlockSpec((None, CONV_T, 1024), lambda b, t: (b, t, Z_CONV // 1024)),
                  const((CONV_K + 1, CONV_WIDTH)), const((1, CONV_WIDTH)), const((1, CONV_WIDTH)),
                  const((1, CONV_WIDTH))],
        out_specs=pl.BlockSpec((None, CONV_T, CONV_WIDTH), lambda b, t: (b, t, 0)),
        out_shape=jax.ShapeDtypeStruct((B, S, CONV_WIDTH), _ACT),
        scratch_shapes=[pltpu.VMEM((CONV_HALO + CONV_T + 8, CONV_WIDTH), F32)],
        compiler_params=_cp("parallel", "arbitrary"),
        name="conformer_conv",
    )(z, w, row(conv_b), row(conv_g), row(conv_beta))


GLA_T = 256
_GLA_LEVELS = (1, 2, 4, 8, 16, 32)


def _gla_masks():
    t = np.arange(GLA_CHUNK)[:, None]
    s = np.arange(GLA_CHUNK)[None, :]
    ms = [(t == s)]
    for c in _GLA_LEVELS:
        ms.append(((t // c) % 2 == 1) & (s // c == t // c - 1))
    m = np.stack(ms).astype(np.float32)
    return np.tile(m, (1, GLA_HEADS, 1))


def _gla_kernel(gq_ref, gk_ref, gv_ref, sm_ref, gr_ref, gw_ref, gb_ref, gg_ref, tril_ref, lm_ref, hm_ref,
                o_ref, state):
    @pl.when(pl.program_id(1) == 0)
    def _():
        state[...] = jnp.zeros(state.shape, F32)

    C = GLA_CHUNK
    tidx = lax.broadcasted_iota(jnp.int32, (C, GLA_HEADS * GLA_DK), 0)

    def stack_heads(x):
        return jnp.concatenate([x * hm_ref[h:h + 1, :] for h in range(GLA_HEADS)], axis=0)

    def chunk_row(bb, r0):
        q = gq_ref[bb, pl.ds(r0, C), :].astype(F32) * (GLA_DK ** -0.5)
        k = gk_ref[bb, pl.ds(r0, C), :].astype(F32)
        v = gv_ref[bb, pl.ds(r0, C), :]
        pre = _dot(sm_ref[bb, pl.ds(r0, C), :], gw_ref[...]) + gb_ref[...]
        la = (jnp.minimum(pre, 0.0) - jnp.log1p(jnp.exp(-jnp.abs(pre)))) * (1.0 / GLA_TAU)
        b = _dot01_left(tril_ref[...], la)
        attn = _dot_nt(stack_heads(q), k) * lm_ref[0]
        bstart = b
        bnext = pltpu.roll(b, C - 1, axis=0)
        for li, c in enumerate(_GLA_LEVELS):
            odd = (tidx // c) % 2 == 1
            q_l = jnp.where(odd, q * jnp.exp(b - bstart), 0.0)
            k_l = jnp.where(odd, 0.0, k * jnp.exp(bnext - b))
            attn = attn + _dot_nt(stack_heads(q_l), k_l) * lm_ref[li + 1]
            half = (tidx % (2 * c)) < c
            bstart = jnp.where(half, bstart, pltpu.roll(bstart, c, axis=0))
            bnext = jnp.where(half, pltpu.roll(bnext, C - c, axis=0), bnext)
        st = state[bb]
        r_intra = _dot(attn, v)
        r_inter = _dot_nt(stack_heads(q * jnp.exp(b)), st)
        b_last = b[C - 1:C, :]
        upd = _dot_tn(v, k * jnp.exp(b_last - b))
        new_st = st * jnp.exp(b_last) + jnp.concatenate(
            [upd[128 * h:128 * h + 128, 64 * h:64 * h + 64] for h in range(GLA_HEADS)], axis=1)
        state[bb] = new_st
        outs = []
        for h in range(GLA_HEADS):
            o = r_intra[64 * h:64 * h + 64, 128 * h:128 * h + 128] + r_inter[64 * h:64 * h + 64, :]
            ms = jnp.mean(o * o, axis=-1, keepdims=True)
            outs.append(o * lax.rsqrt(ms + 1e-6) * gg_ref[...])
        gr = gr_ref[bb, pl.ds(r0, C), :].astype(F32)
        o_ref[bb, pl.ds(r0, C), :] = (jnp.concatenate(outs, axis=1) * (gr * _sigmoid(gr))).astype(o_ref.dtype)

    def chunk(ci, carry):
        r0 = pl.multiple_of(ci * C, C)
        for bb in range(gq_ref.shape[0]):
            chunk_row(bb, r0)
        return carry

    lax.fori_loop(0, GLA_T // C, chunk, 0)


def _gla(z, gla_w, gla_b, gla_g):
    B, S, _ = z.shape
    gw = jnp.zeros((128, GLA_HEADS * GLA_DK), F32).at[SMALL_GA:SMALL_GA + GLA_RANK].set(gla_w).astype(_MM)
    tril = jnp.asarray(np.tril(np.ones((GLA_CHUNK, GLA_CHUNK), np.float32)), dtype=_MM)
    lm = jnp.asarray(_gla_masks())
    hm = jnp.asarray(np.repeat(np.eye(GLA_HEADS, dtype=np.float32), GLA_DK, axis=1))
    const = lambda shape: pl.BlockSpec(shape, lambda b, t: (0,) * len(shape))
    nb = 2 if B % 2 == 0 else 1
    zspec = lambda w, off: pl.BlockSpec((nb, GLA_T, w), lambda b, t: (b, t, off // w))
    return pl.pallas_call(
        _gla_kernel,
        grid=(B // nb, S // GLA_T),
        in_specs=[zspec(256, Z_GQ), zspec(256, Z_GK), zspec(512, Z_GV), zspec(128, Z_SMALL), zspec(512, Z_GR),
                  const((128, 256)), const((1, 256)), const((1, GLA_DV)), const((GLA_CHUNK, GLA_CHUNK)),
                  const((7, GLA_HEADS * GLA_CHUNK, GLA_CHUNK)), const((GLA_HEADS, 256))],
        out_specs=pl.BlockSpec((nb, GLA_T, 512), lambda b, t: (b, t, 0)),
        out_shape=jax.ShapeDtypeStruct((B, S, 512), _ACT),
        scratch_shapes=[pltpu.VMEM((nb, GLA_DV, GLA_HEADS * GLA_DK), F32)],
        compiler_params=_cp("parallel", "arbitrary"),
        name="gla",
    )(z, z, z, z, z, gw, gla_b.reshape(1, -1).astype(F32), gla_g.reshape(1, -1).astype(F32), tril, lm, hm)


MERGE_T = 512


def _merge_kernel(oa_ref, ob_ref, oc_ref, mg_ref, wb_ref, wo_ref, x_ref, g_ref, b_ref, o_ref):
    merged = None
    for j, r in enumerate((oa_ref, ob_ref, oc_ref)):
        gate = _sigmoid(mg_ref[:, D_MODEL * j:D_MODEL * (j + 1)].astype(F32))
        term = gate * _dot(r[...], wb_ref[j])
        merged = term if merged is None else merged + term
    mix = _dot(merged, wo_ref[...])
    o_ref[...] = _layer_norm(DEEPNORM_ALPHA * x_ref[...] + mix, g_ref[...], b_ref[...])


def _merge(oa, ob, oc, z2, w_branch, w_out, x2, g, b):
    n = x2.shape[0]
    tok = lambda w, blk=0: pl.BlockSpec((MERGE_T, w), lambda i: (i, blk))
    const = lambda shape: pl.BlockSpec(shape, lambda i: (0,) * len(shape))
    return pl.pallas_call(
        _merge_kernel,
        grid=(n // MERGE_T,),
        in_specs=[tok(512), tok(512), tok(512), tok(3072, Z_MERGE // 3072), const((3, 512, D_MODEL)),
                  const((D_MODEL, D_MODEL)), tok(D_MODEL), const((1, D_MODEL)), const((1, D_MODEL))],
        out_specs=tok(D_MODEL),
        out_shape=jax.ShapeDtypeStruct((n, D_MODEL), F32),
        compiler_params=_cp("parallel"),
        name="merge_outproj_ln",
    )(oa, ob, oc, z2, w_branch.astype(_MM), w_out.astype(_MM), x2, g.reshape(1, -1), b.reshape(1, -1))


XA_T = 512


ROW_TILE = 8


def _store_row_tiles(ref, y):
    n = y.shape[0]
    for s in range(ROW_TILE):
        ref[pl.ds(s, n, stride=ROW_TILE), :] = y[:, 128 * s:128 * (s + 1)]


def _load_row_tiles(ref):
    n = ref.shape[0] // ROW_TILE
    return jnp.concatenate([ref[pl.ds(s, n, stride=ROW_TILE), :] for s in range(ROW_TILE)], axis=1)


def _xattn_kernel(x_ref, kv_ref, wq_ref, wo_ref, g_ref, b_ref, o_ref, o3_ref):
    x = x_ref[...]
    q = (_dot(x, wq_ref[...]) * (XA_DH ** -0.5)).astype(_MM)
    heads = []
    for h in range(XA_HEADS):
        hs = slice(XA_DH * h, XA_DH * (h + 1))
        s = _dot_nt(q[:, hs], kv_ref[:, hs])
        e = jnp.exp(s - jnp.max(s, axis=-1, keepdims=True))
        p = e / jnp.sum(e, axis=-1, keepdims=True)
        heads.append(_dot(p, kv_ref[:, D_MODEL + XA_DH * h:D_MODEL + XA_DH * (h + 1)]))
    xa = _dot(jnp.concatenate(heads, axis=1), wo_ref[...])
    y = _layer_norm(DEEPNORM_ALPHA * x + xa, g_ref[...], b_ref[...])
    o_ref[...] = y
    _store_row_tiles(o3_ref, y)


def _cross_attention(x, kv, wq, wo, g, b):
    B, S, _ = x.shape
    M = kv.shape[1]
    nt = S // XA_T
    const = lambda shape: pl.BlockSpec(shape, lambda bb, t: (0,) * len(shape))
    return pl.pallas_call(
        _xattn_kernel,
        grid=(B, nt),
        in_specs=[pl.BlockSpec((None, XA_T, D_MODEL), lambda bb, t: (bb, t, 0)),
                  pl.BlockSpec((None, M, 2 * D_MODEL), lambda bb, t: (bb, 0, 0)),
                  const((D_MODEL, D_MODEL)), const((D_MODEL, D_MODEL)), const((1, D_MODEL)), const((1, D_MODEL))],
        out_specs=[pl.BlockSpec((None, XA_T, D_MODEL), lambda bb, t: (bb, t, 0)),
                   pl.BlockSpec((XA_T * ROW_TILE, 128), lambda bb, t: (bb * nt + t, 0))],
        out_shape=[jax.ShapeDtypeStruct((B, S, D_MODEL), F32), jax.ShapeDtypeStruct((B * S * ROW_TILE, 128), F32)],
        compiler_params=_cp("parallel", "parallel"),
        name="cross_attention_ln",
    )(x, kv, wq.astype(_MM), wo.astype(_MM), g.reshape(1, -1), b.reshape(1, -1))


ROUTE_T = 512


def _router_kernel(x_ref, wt_ref, b_ref, upper_ref, ti_ref, gate_ref, rank_ref, cnt_ref, carry):
    @pl.when(pl.program_id(0) == 0)
    def _():
        carry[...] = jnp.zeros(carry.shape, F32)

    x = x_ref[...]
    xh = x.astype(_MM)
    xl = (x - xh.astype(F32)).astype(_MM)
    w = wt_ref[...]
    wh = w.astype(_MM)
    wl = (w - wh.astype(F32)).astype(_MM)
    logits = _dot_nt(wh, xh) + _dot_nt(wh, xl) + _dot_nt(wl, xh) + b_ref[...]
    eidx = lax.broadcasted_iota(jnp.int32, logits.shape, 0)
    v = logits
    tops, idxs, hots = [], [], []
    for _ in range(TOP_K):
        m = jnp.max(v, axis=0, keepdims=True)
        idx = jnp.min(jnp.where(v == m, eidx, N_EXPERTS), axis=0, keepdims=True)
        hot = eidx == idx
        v = jnp.where(hot, -jnp.inf, v)
        tops.append(m)
        idxs.append(idx)
        hots.append(jnp.where(hot, 1.0, 0.0))
    es = [jnp.exp(t - tops[0]) for t in tops]
    den = es[0] + es[1] + es[2] + es[3]
    multihot = hots[0] + hots[1] + hots[2] + hots[3]
    before = jnp.dot(multihot.astype(_MM), upper_ref[...], preferred_element_type=F32) + carry[...][:, 0:1]
    ranks = [jnp.sum(h * before, axis=0, keepdims=True) for h in hots]
    pad = jnp.zeros((8 - TOP_K, x.shape[0]), F32)
    ti_ref[...] = jnp.concatenate(idxs + [pad.astype(jnp.int32)], axis=0)
    gate_ref[...] = jnp.concatenate([e / den for e in es] + [pad], axis=0)
    rank_ref[...] = jnp.concatenate(ranks + [pad], axis=0).astype(jnp.int32)
    carry[...] = carry[...] + jnp.sum(multihot, axis=1, keepdims=True)
    cnt_ref[...] = carry[...]


def _router(x2, router_w, router_b):
    n = x2.shape[0]
    upper = jnp.asarray(np.triu(np.ones((ROUTE_T, ROUTE_T), np.float32), 1), dtype=_MM)
    bcol = jnp.broadcast_to(router_b.astype(F32)[:, None], (N_EXPERTS, ROUTE_T))
    tokspec = pl.BlockSpec((8, ROUTE_T), lambda i: (0, i))
    const = lambda shape: pl.BlockSpec(shape, lambda i: (0,) * len(shape))
    return pl.pallas_call(
        _router_kernel,
        grid=(n // ROUTE_T,),
        in_specs=[pl.BlockSpec((ROUTE_T, D_MODEL), lambda i: (i, 0)), const((N_EXPERTS, D_MODEL)),
                  const((N_EXPERTS, ROUTE_T)), const((ROUTE_T, ROUTE_T))],
        out_specs=[tokspec, tokspec, tokspec, const((N_EXPERTS, 128))],
        out_shape=[jax.ShapeDtypeStruct((8, n), jnp.int32), jax.ShapeDtypeStruct((8, n), F32),
                   jax.ShapeDtypeStruct((8, n), jnp.int32), jax.ShapeDtypeStruct((N_EXPERTS, 128), F32)],
        scratch_shapes=[pltpu.VMEM((N_EXPERTS, 128), F32)],
        compiler_params=_cp("arbitrary"),
        name="moe_router",
    )(x2, router_w.T.astype(F32), bcol, upper)


def _moe_kernel(blk_e_ref, nvalid_ref, tokc_ref, tokn_ref, dstp_ref, x_hbm, wgu_ref, bgu_ref, wdn_ref, bdn_ref,
                y_hbm, xbuf, ybuf, wgu_mm, wdn_mm, sem_in, sem_out):
    b = pl.program_id(0)
    nvalid = nvalid_ref[0]
    slot = b % 2
    other = 1 - slot

    def gather_row(tok_ref, s, r, priority=0):
        src = x_hbm.at[pl.ds(pl.multiple_of(tok_ref[0, r], ROW_TILE), ROW_TILE), :]
        pltpu.make_async_copy(src, xbuf.at[s, pl.ds(ROW_TILE * r, ROW_TILE), :], sem_in.at[s]).start(priority)

    def scatter_row(s, r, priority=0):
        dst = y_hbm.at[pl.ds(pl.multiple_of(dstp_ref[0, r], ROW_TILE), ROW_TILE), :]
        pltpu.make_async_copy(ybuf.at[s, pl.ds(ROW_TILE * r, ROW_TILE), :], dst, sem_out.at[s]).start(priority)

    def wait_in(s):
        pltpu.make_async_copy(x_hbm.at[pl.ds(0, MOE_BLK * ROW_TILE), :], xbuf.at[s], sem_in.at[s]).wait()

    def wait_out(s):
        pltpu.make_async_copy(ybuf.at[s], y_hbm.at[pl.ds(0, MOE_BLK * ROW_TILE), :], sem_out.at[s]).wait()

    @pl.when(b == 0)
    def _():
        ybuf[...] = jnp.zeros(ybuf.shape, F32)

        def body(r, c):
            gather_row(tokc_ref, 0, r)
            return c
        lax.fori_loop(0, MOE_BLK, body, 0)

    @pl.when(jnp.logical_and(b >= 1, b <= nvalid))
    def _():
        wait_out(slot)

    @pl.when(jnp.logical_and(b < nvalid, jnp.logical_or(b == 0, blk_e_ref[b] != blk_e_ref[jnp.maximum(b - 1, 0)])))
    def _():
        wgu_mm[...] = wgu_ref[...].astype(_MM)
        wdn_mm[...] = wdn_ref[...].astype(_MM)

    @pl.when(b < nvalid)
    def _():
        wait_in(slot)
        xb = _load_row_tiles(xbuf.at[slot]).astype(_MM)
        for r in range(MOE_BLK):
            gather_row(tokn_ref, other, r, r % 2)
        gu = jnp.dot(xb, wgu_mm[...], preferred_element_type=F32) + bgu_ref[...]
        for r in range(MOE_BLK):
            scatter_row(other, r, r % 2)
        g = jnp.minimum(gu[:, :D_MODEL], SWIGLU_LIMIT)
        u = jnp.clip(gu[:, D_MODEL:], -SWIGLU_LIMIT, SWIGLU_LIMIT)
        h = (u + 1.0) * g * _sigmoid(SWIGLU_ALPHA * g)
        _store_row_tiles(ybuf.at[slot], jnp.dot(h.astype(_MM), wdn_mm[...], preferred_element_type=F32) + bdn_ref[...])

    @pl.when(b == nvalid)
    def _():
        def body(r, c):
            scatter_row(other, r)
            return c
        lax.fori_loop(0, MOE_BLK, body, 0)
        wait_out(other)
        wait_in(slot)


def _moe_experts(x3, slot_tok, slot_dst, blk_e, nvalid, n_out_rows, layer, w_gu, b_gu, w_dn, b_dn):
    n_blocks = blk_e.shape[0]
    tok3 = (slot_tok * ROW_TILE).reshape(n_blocks, 1, MOE_BLK)
    dst3 = (slot_dst * ROW_TILE).reshape(n_blocks + 1, 1, MOE_BLK)
    smem = lambda f: pl.BlockSpec((None, 1, MOE_BLK), f, memory_space=pltpu.SMEM)
    wspec = lambda r, c: pl.BlockSpec((None, None, r, c), lambda b, be, nv: (layer, be[b], 0, 0))
    grid_spec = pltpu.PrefetchScalarGridSpec(
        num_scalar_prefetch=2,
        grid=(n_blocks,),
        in_specs=[
            smem(lambda b, be, nv: (b, 0, 0)),
            smem(lambda b, be, nv: (jnp.minimum(b + 1, n_blocks - 1), 0, 0)),
            smem(lambda b, be, nv: (b, 0, 0)),
            pl.BlockSpec(memory_space=pl.ANY),
            wspec(D_MODEL, 2 * D_MODEL), wspec(1, 2 * D_MODEL), wspec(D_MODEL, D_MODEL), wspec(1, D_MODEL),
        ],
        out_specs=pl.BlockSpec(memory_space=pl.ANY),
        scratch_shapes=[pltpu.VMEM((2, MOE_BLK * ROW_TILE, 128), F32), pltpu.VMEM((2, MOE_BLK * ROW_TILE, 128), F32),
                        pltpu.VMEM((D_MODEL, 2 * D_MODEL), _MM), pltpu.VMEM((D_MODEL, D_MODEL), _MM),
                        pltpu.SemaphoreType.DMA((2,)), pltpu.SemaphoreType.DMA((2,))],
    )
    return pl.pallas_call(
        _moe_kernel,
        grid_spec=grid_spec,
        out_shape=jax.ShapeDtypeStruct((n_out_rows * ROW_TILE, 128), F32),
        compiler_params=_cp("arbitrary"),
        name="moe_experts",
    )(blk_e, nvalid, tok3, tok3, dst3, x3, w_gu, b_gu[:, :, None, :], w_dn, b_dn[:, :, None, :])


def _combine_kernel(y0_ref, y1_ref, y2_ref, y3_ref, gate_ref, x_ref, g_ref, b_ref, o_ref):
    gate = gate_ref[...]
    ff = gate[:, 0:1] * _load_row_tiles(y0_ref)
    for k, r in enumerate((y1_ref, y2_ref, y3_ref)):
        ff = ff + gate[:, k + 1:k + 2] * _load_row_tiles(r)
    o_ref[...] = _layer_norm(DEEPNORM_ALPHA * x_ref[...] + ff, g_ref[...], b_ref[...])


def _moe_combine(ys, gate_t, x2, g, b):
    n = x2.shape[0]
    steps = n // COMB_T
    const = lambda shape: pl.BlockSpec(shape, lambda i: (0,) * len(shape))
    yspec = lambda k: pl.BlockSpec((COMB_T * ROW_TILE, 128), lambda i: (k * steps + i, 0))
    return pl.pallas_call(
        _combine_kernel,
        grid=(steps,),
        in_specs=[yspec(0), yspec(1), yspec(2), yspec(3), pl.BlockSpec((COMB_T, 8), lambda i: (i, 0)),
                  pl.BlockSpec((COMB_T, D_MODEL), lambda i: (i, 0)), const((1, D_MODEL)), const((1, D_MODEL))],
        out_specs=pl.BlockSpec((COMB_T, D_MODEL), lambda i: (i, 0)),
        out_shape=jax.ShapeDtypeStruct((n, D_MODEL), F32),
        compiler_params=_cp("parallel"),
        name="moe_combine_ln",
    )(ys, ys, ys, ys, gate_t, x2, g.reshape(1, -1), b.reshape(1, -1))


def _moe_ffn(x2, x3, router_w, router_b, layer, w_gu, b_gu, w_dn, b_dn, g, b):
    n = x2.shape[0]
    a = n * TOP_K
    top_i, gate, rank, cnt = _router(x2, router_w, router_b)
    counts = cnt[:, 0].astype(jnp.int32)
    padded = (counts + MOE_BLK - 1) // MOE_BLK * MOE_BLK
    pends = jnp.cumsum(padded)
    pstarts = pends - padded
    n_blocks = a // MOE_BLK + N_EXPERTS
    n_slots = n_blocks * MOE_BLK
    eids = jnp.arange(N_EXPERTS, dtype=jnp.int32)
    hot = top_i[:TOP_K, None, :] == eids[None, :, None]
    dest = jnp.sum(jnp.where(hot, pstarts[None, :, None], 0), axis=1) + rank[:TOP_K]
    spare = a + jnp.arange(n_slots, dtype=jnp.int32) % MOE_BLK
    slot_asg = spare.at[dest.reshape(-1)].set(jnp.arange(a, dtype=jnp.int32))
    slot_tok = jnp.where(slot_asg < a, slot_asg % n, 0)
    slot_dst = jnp.concatenate([spare[:MOE_BLK], slot_asg])
    blk_start = jnp.arange(n_blocks, dtype=jnp.int32) * MOE_BLK
    blk_e = jnp.minimum(jnp.sum((pends[None, :] <= blk_start[:, None]).astype(jnp.int32), axis=1), N_EXPERTS - 1)
    nvalid = (pends[-1:] // MOE_BLK).astype(jnp.int32)
    ys = _moe_experts(x3, slot_tok, slot_dst, blk_e, nvalid, a + MOE_BLK, layer, w_gu, b_gu, w_dn, b_dn)
    return _moe_combine(ys, gate.T, x2, g, b)


def _layer(x, mem, tables, w_in, cmp_pe, cmp_w1, cmp_b1, cmp_w2, conv_w, conv_b, conv_g, conv_beta, gla_w, gla_b,
           gla_g, w_branch, w_out, xa_wq, xa_wkv, xa_wo, router_w, router_b, layer, w_gu, b_gu, w_dn, b_dn, ng, nb):
    B, S, D = x.shape
    n = B * S
    x2 = x.reshape(n, D)
    z2 = _matmul(x2, _reorder_w_in(w_in), 1024, ZW // 5, _ACT)
    z = z2.reshape(B, S, ZW)
    kvc = z[:, :, Z_KV:Z_KV + 256].reshape(B, S // 16, 16, 2, 128).transpose(3, 0, 1, 2, 4)
    cmp = _compress(kvc.reshape(2, B, S // 16, 2048), cmp_pe, cmp_w1, cmp_b1, cmp_w2)
    kv4 = jnp.pad(z[:, :, Z_KV + 256:Z_KV + 768], ((0, 0), (WINDOW, 0), (0, 0)))
    oa = _nsa_attention(z, cmp[0], cmp[1], kv4, tables)
    ob = _conformer_conv(z, conv_w, conv_b, conv_g, conv_beta)
    oc = _gla(z, gla_w, gla_b, gla_g)
    x2 = _merge(oa.reshape(n, 512), ob.reshape(n, 512), oc.reshape(n, 512), z2, w_branch, w_out, x2, ng[0], nb[0])
    kv = _matmul(mem.reshape(-1, D), xa_wkv.astype(_MM), 1024, 1024, _ACT).reshape(B, -1, 2 * D)
    x2, x3 = _cross_attention(x2.reshape(B, S, D), kv, xa_wq, xa_wo, ng[1], nb[1])
    x2 = _moe_ffn(x2.reshape(n, D), x3, router_w, router_b, layer, w_gu, b_gu, w_dn, b_dn, ng[2], nb[2])
    return x2.reshape(B, S, D)


def kernel(x, mem, rel_bias, w_in, cmp_pe, cmp_w1, cmp_b1, cmp_w2, conv_w, conv_b, conv_norm_g, conv_norm_b,
           gla_gate_w, gla_gate_b, gla_norm_g, w_branch, w_out, xa_wq, xa_wkv, xa_wo, router_w, router_b,
           expert_w_gu, expert_b_gu, expert_w_down, expert_b_down, norm_g, norm_b):
    tables = _nsa_tables(rel_bias)
    for l in range(DEPTH):
        x = _layer(x, mem, tables, w_in[l], cmp_pe[l], cmp_w1[l], cmp_b1[l], cmp_w2[l], conv_w[l], conv_b[l],
                   conv_norm_g[l], conv_norm_b[l], gla_gate_w[l], gla_gate_b[l], gla_norm_g[l], w_branch[l],
                   w_out[l], xa_wq[l], xa_wkv[l], xa_wo[l], router_w[l], router_b[l], l, expert_w_gu,
                   expert_b_gu, expert_w_down, expert_b_down, norm_g[l], norm_b[l])
    return x
```

```python
import functools
import math

import numpy as np
import jax
import jax.numpy as jnp
from jax import lax
from jax.experimental import pallas as pl
from jax.experimental.pallas import tpu as pltpu

F32 = jnp.float32
_MM = jnp.bfloat16
_ACT = jnp.bfloat16
_SM = jnp.float32
NEG = -1e30
LOG2E = 1.4426950408889634

D_MODEL = 1024
DEPTH = 2
MEM_LEN = 256
NSA_HEADS = 8
NSA_DH = 64
CMP_LEN = 32
CMP_STRIDE = 16
SEL_BLOCK = 64
SEL_TOP = 16
WINDOW = 512
CONV_WIDTH = 512
CONV_K = 31
GLA_HEADS = 4
GLA_DK = 64
GLA_DV = 128
GLA_RANK = 16
GLA_TAU = 16.0
GLA_CHUNK = 64
REL_BUCKETS = 32
REL_MAX_DIST = 128
XA_HEADS = 4
XA_DH = 256
N_EXPERTS = 32
TOP_K = 4
SWIGLU_ALPHA = 1.702
SWIGLU_LIMIT = 7.0
DEEPNORM_ALPHA = (2 * DEPTH) ** 0.25

Z_MERGE = 0
Z_CONV = 3072
Z_Q = 4096
Z_GV = 4608
Z_GR = 5120
Z_KV = 5632
Z_GQ = 6400
Z_GK = 6656
Z_SMALL = 6912
ZW = 7040
SMALL_GA = 24

_IN_SIZES = (512, 768, 24, 1024, 256, 256, 512, 16, 512, 3072)
_IN_OFF = np.concatenate([[0], np.cumsum(_IN_SIZES)]).astype(int)

VMEM_LIMIT = 56 * 1024 * 1024

NSA_QB = 128
MOE_BLK = 512
COMB_T = 256


def _cp(*sem):
    return pltpu.CompilerParams(dimension_semantics=sem, vmem_limit_bytes=VMEM_LIMIT)


def _dot(a, b):
    return jnp.dot(a.astype(_MM), b.astype(_MM), preferred_element_type=F32)


def _dot_nt(a, b):
    return lax.dot_general(a.astype(_MM), b.astype(_MM), (((1,), (1,)), ((), ())), preferred_element_type=F32)


def _dot_tn(a, b):
    return lax.dot_general(a.astype(_MM), b.astype(_MM), (((0,), (0,)), ((), ())), preferred_element_type=F32)


def _split3(x):
    x1 = x.astype(_MM)
    r1 = x - x1.astype(F32)
    x2 = r1.astype(_MM)
    x3 = (r1 - x2.astype(F32)).astype(_MM)
    return x1, x2, x3


def _dot01_left(m01, x):
    x1, x2, x3 = _split3(x)
    return (jnp.dot(m01, x1, preferred_element_type=F32) + jnp.dot(m01, x2, preferred_element_type=F32)
            + jnp.dot(m01, x3, preferred_element_type=F32))


def _layer_norm(y, g, b, eps=1e-5):
    mu = jnp.mean(y, axis=-1, keepdims=True)
    d = y - mu
    var = jnp.mean(d * d, axis=-1, keepdims=True)
    return d * lax.rsqrt(var + eps) * g + b


def _sigmoid(x):
    return 1.0 / (1.0 + jnp.exp(-x))


def _mm_kernel(x_ref, w_ref, o_ref):
    o_ref[...] = _dot(x_ref[...], w_ref[...]).astype(o_ref.dtype)


def _matmul(x, w, tm, tn, out_dtype):
    m, k = x.shape
    n = w.shape[1]
    tm = min(tm, m)
    return pl.pallas_call(
        _mm_kernel,
        grid=(m // tm, n // tn),
        in_specs=[pl.BlockSpec((tm, k), lambda i, j: (i, 0)), pl.BlockSpec((k, tn), lambda i, j: (0, j))],
        out_specs=pl.BlockSpec((tm, tn), lambda i, j: (i, j)),
        out_shape=jax.ShapeDtypeStruct((m, n), out_dtype),
        compiler_params=_cp("parallel", "parallel"),
        name="matmul",
    )(x, w)


def _reorder_w_in(w_in):
    def cols(i):
        return w_in[:, _IN_OFF[i]:_IN_OFF[i + 1]]
    small = jnp.concatenate([cols(2), cols(7), jnp.zeros((D_MODEL, 128 - 40), w_in.dtype)], axis=1)
    w = jnp.concatenate([cols(9), cols(3), cols(0), cols(6), cols(8), cols(1), cols(4), cols(5), small], axis=1)
    return w.astype(_MM)


def _cmp_kernel(x_ref, pea_ref, peb_ref, wa_ref, wb_ref, b1_ref, w2_ref, o_ref):
    x = x_ref[...].astype(F32)
    a = _dot(x + pea_ref[...], wa_ref[...])
    bm = _dot(x + peb_ref[...], wb_ref[...])
    n = bm.shape[0]
    bs = pltpu.roll(bm, n - 1, axis=0)
    h = jax.nn.gelu(a + bs + b1_ref[...], approximate=True)
    o_ref[...] = _dot(h, w2_ref[...]).astype(o_ref.dtype)


def _compress(kv2, cmp_pe, cmp_w1, cmp_b1, cmp_w2):
    _, B, R, _ = kv2.shape
    eye2 = jnp.eye(2, dtype=F32)

    def half(w1h):
        w = jnp.einsum('rlde,gh->rlgdhe', w1h, eye2)
        return w.reshape(2, 16 * 128, 128).astype(_MM)

    wa = half(cmp_w1[:, :16])
    wb = half(cmp_w1[:, 16:])

    def pe_half(p):
        return jnp.broadcast_to(p[:, :, None, :], (2, 16, 2, 64)).reshape(2, 1, 2048).astype(F32)

    pea = pe_half(cmp_pe[:, :16])
    peb = pe_half(cmp_pe[:, 16:])
    b1 = jnp.tile(cmp_b1, (1, 2)).reshape(2, 1, 128).astype(F32)
    w2 = jnp.einsum('rde,gh->rgdhe', cmp_w2, eye2).reshape(2, 128, 128).astype(_MM)
    return pl.pallas_call(
        _cmp_kernel,
        grid=(2, B),
        in_specs=[
            pl.BlockSpec((None, None, R, 2048), lambda r, b: (r, b, 0, 0)),
            pl.BlockSpec((None, 1, 2048), lambda r, b: (r, 0, 0)),
            pl.BlockSpec((None, 1, 2048), lambda r, b: (r, 0, 0)),
            pl.BlockSpec((None, 2048, 128), lambda r, b: (r, 0, 0)),
            pl.BlockSpec((None, 2048, 128), lambda r, b: (r, 0, 0)),
            pl.BlockSpec((None, 1, 128), lambda r, b: (r, 0, 0)),
            pl.BlockSpec((None, 128, 128), lambda r, b: (r, 0, 0)),
        ],
        out_specs=pl.BlockSpec((None, None, R, 128), lambda r, b: (r, b, 0, 0)),
        out_shape=jax.ShapeDtypeStruct((2, B, R, 128), _ACT),
        compiler_params=_cp("parallel", "parallel"),
        name="nsa_compress",
    )(kv2, pea, peb, wa, wb, b1, w2)


def _t5_bucket_np(dist):
    n = np.maximum(dist, 0)
    exact = REL_BUCKETS // 2
    lr = np.log(np.maximum(n, 1).astype(np.float32) / np.float32(exact)) / np.float32(math.log(REL_MAX_DIST / exact))
    large = exact + (lr * np.float32(REL_BUCKETS - exact)).astype(np.int32)
    return np.where(n < exact, n, np.minimum(large, REL_BUCKETS - 1))


def _nsa_tables(rel_bias):
    rel = (rel_bias - rel_bias[REL_BUCKETS - 1:REL_BUCKETS]).astype(F32) * LOG2E
    q = np.arange(NSA_QB)

    def table(dist, valid, fill):
        hot = jnp.asarray(_t5_bucket_np(dist), jnp.int32)[:, :, None] == jnp.arange(REL_BUCKETS)[None, None, :]
        t = jnp.sum(jnp.where(hot[..., None], rel[None, None], 0.0), axis=2)
        t = jnp.where(valid[:, :, None], t, fill)
        return jnp.transpose(t, (0, 2, 1)).reshape(dist.shape[0], NSA_HEADS * NSA_QB)

    cc = np.arange(24) - 16
    d_c = q[None, :] - CMP_STRIDE * cc[:, None] - (CMP_LEN - 1)
    t_cmp = table(d_c, d_c >= 0, 0.0)
    ko = np.arange(256)
    d_n = q[None, :] + 128 - ko[:, None]
    t_near = table(d_n, d_n >= 0, NEG)
    kw = np.arange(WINDOW + NSA_QB)
    d_w = q[None, :] + WINDOW - kw[:, None]
    in_win = (d_w >= 0) & (d_w < WINDOW)
    t_win = jnp.stack([table(d_w, in_win & (kw[:, None] >= WINDOW - NSA_QB * v), NEG) for v in range(5)])
    return t_cmp, t_near, t_win


def _nsa_kernel(q_ref, gate_ref, kcmp_ref, vcmp_ref, ksel_ref, vsel_ref, kwin_ref, vwin_ref,
                cmpa_ref, tcmp_ref, tnear_ref, twin_ref, ovl_ref, out_ref, s_scr, v_scr, sel_scr, far_scr):
    i = pl.program_id(1)
    ncmp = kcmp_ref.shape[0]
    nblk = ovl_ref.shape[0]
    qt = (q_ref[...].astype(F32) * (NSA_DH ** -0.5 * LOG2E)).T
    gt = _sigmoid(gate_ref[...].astype(F32)).T
    zeros64 = jnp.zeros((64, NSA_QB), F32)
    jidx = lax.broadcasted_iota(jnp.int32, (nblk, NSA_QB), 0)
    qidx = lax.broadcasted_iota(jnp.int32, (nblk, NSA_QB), 1)
    sub8 = lax.broadcasted_iota(jnp.int32, (8, NSA_QB), 0)
    cur = 2 * i + (qidx >= SEL_BLOCK).astype(jnp.int32)
    forced = (jidx == 0) | (jidx == cur) | (jidx == cur - 1)
    future = jidx > cur
    neg8 = jnp.full((8, NSA_QB), NEG, F32)
    s_scr[0:16, :] = jnp.zeros((16, 4 * NSA_QB), F32)

    def mask_rows(scr, first_block, n):
        rows = [jnp.broadcast_to(scr[pl.ds(8 + first_block + b, 1), :], (SEL_BLOCK, NSA_QB)) for b in range(n)]
        return jnp.concatenate(rows, axis=0)

    def add_mask(s, m):
        return jnp.concatenate([s[:, 128 * h:128 * h + 128] + m for h in range(4)], axis=1)

    def col_max(x):
        if x.dtype == F32:
            return jnp.max(x, axis=0, keepdims=True)
        y = jnp.max(x.reshape(x.shape[0] // 16, 16, x.shape[1]), axis=0)
        return jnp.max(y.astype(F32), axis=0, keepdims=True)

    def col_sum(e):
        if e.dtype == F32:
            return jnp.sum(e, axis=0, keepdims=True)
        return jnp.dot(jnp.ones((8, e.shape[0]), e.dtype), e, preferred_element_type=F32)[0:1]

    def probs(s, table, mask, m_old=None):
        x = s.astype(_SM)
        if table is not None:
            x = x + table
        if mask is not None:
            x = add_mask(x, mask.astype(_SM))
        m = col_max(x)
        if m_old is not None:
            m = jnp.maximum(m_old, m)
        e = jnp.exp2(x - m.astype(_SM))
        return m, e, col_sum(e)

    q_gs, o_cs, o_ws, state = [], [], [], []
    for g in range(2):
        blocks = []
        for h in range(4):
            r = qt[64 * (4 * g + h):64 * (4 * g + h) + 64]
            blocks.append(jnp.concatenate([r, zeros64] if g == 0 else [zeros64, r], axis=0))
        q_g = jnp.concatenate(blocks, axis=1).astype(_MM)
        q_gs.append(q_g)
        cs = slice(512 * g, 512 * g + 512)

        s_scr[16:16 + ncmp, :] = _dot(kcmp_ref[...], q_g)
        w0 = pl.multiple_of(8 * i, 8)
        s_scr[pl.ds(w0, 24), :] = s_scr[pl.ds(w0, 24), :] + tcmp_ref[:, cs]
        sc = jnp.where(cmpa_ref[...] <= NSA_QB * i, s_scr[16:16 + ncmp, :], NEG)
        m = jnp.maximum(jnp.max(sc, axis=0, keepdims=True), 0.1 * NEG)
        e = jnp.exp2(sc - m)
        l = jnp.sum(e, axis=0, keepdims=True)
        pc = e / jnp.maximum(l, 1e-30)
        o_cs.append(_dot_tn(vcmp_ref[...], pc))

        psum = pc[:, 0:128] + pc[:, 128:256] + pc[:, 256:384] + pc[:, 384:512]
        imp = _dot01_left(ovl_ref[...], psum)
        v = jnp.where(forced, 1e30, jnp.where(future, -1.0, imp))
        v_scr[g] = v
        sel_scr[g, 0:8, :] = neg8
        far_scr[g, 0:8, :] = neg8
        for r8 in range(0, nblk, 8):
            vr = v[r8:r8 + 8]
            cnt = jnp.zeros((8, NSA_QB), F32)
            for jp in range(nblk):
                row = jnp.broadcast_to(v_scr[g, pl.ds(jp, 1), :], (8, NSA_QB))
                if jp < r8:
                    cnt = cnt + jnp.where(row >= vr, 1.0, 0.0)
                elif jp >= r8 + 8:
                    cnt = cnt + jnp.where(row > vr, 1.0, 0.0)
                else:
                    cnt = cnt + jnp.where(sub8 > jp - r8, jnp.where(row >= vr, 1.0, 0.0),
                                          jnp.where(row > vr, 1.0, 0.0))
            sel = jnp.where(cnt < float(min(SEL_TOP, nblk)), 0.0, NEG)
            sel_scr[g, 8 + r8:16 + r8, :] = sel
            far_scr[g, 8 + r8:16 + r8, :] = jnp.where(jidx[r8:r8 + 8] <= 2 * i - 3, sel, NEG)

        n0 = pl.multiple_of(NSA_QB * i + WINDOW - 128, 128)
        m_s, e, l_s = probs(_dot(ksel_ref[pl.ds(n0, 256), :], q_g), tnear_ref[:, cs],
                            mask_rows(sel_scr.at[g], 2 * i - 2, 4))
        state += [m_s, l_s, _dot_tn(vsel_ref[pl.ds(n0, 256), :], e)]

        w_start = pl.multiple_of(NSA_QB * i, 128)
        _, e, l_w = probs(_dot(kwin_ref[pl.ds(w_start, WINDOW + NSA_QB), :], q_g), twin_ref[:, cs], None)
        o_ws.append(_dot_tn(vwin_ref[pl.ds(w_start, WINDOW + NSA_QB), :], e) / l_w)

    def far_body(c, carry):
        k0 = pl.multiple_of(WINDOW + 512 * c, 512)
        kf = ksel_ref[pl.ds(k0, 512), :]
        vf = vsel_ref[pl.ds(k0, 512), :]
        new = []
        for g in range(2):
            m_o, l_o, a_o = carry[3 * g:3 * g + 3]
            m_n, ef, l_c = probs(_dot(kf, q_gs[g]), None, mask_rows(far_scr.at[g], 8 * c, 8), m_o)
            alpha = jnp.exp2(m_o - m_n)
            new += [m_n, alpha * l_o + l_c, alpha * a_o + _dot_tn(vf, ef)]
        return tuple(new)

    state = lax.fori_loop(0, (2 * i + 5) // 8, far_body, tuple(state))

    out_rows = []
    for g in range(2):
        o_s = state[3 * g + 2] / state[3 * g + 1]
        for h in range(4):
            hs = slice(128 * h, 128 * h + 128)
            ds_ = slice(64 * g, 64 * g + 64)
            gi = (4 * g + h) * 3
            out_rows.append(gt[gi:gi + 1, :] * o_cs[g][ds_, hs] + gt[gi + 1:gi + 2, :] * o_s[ds_, hs]
                            + gt[gi + 2:gi + 3, :] * o_ws[g][ds_, hs])
    out_ref[...] = jnp.concatenate(out_rows, axis=0).T.astype(out_ref.dtype)


def _nsa_attention(z, kcmp, vcmp, kv4, tables):
    B, S, _ = z.shape
    n_steps = S // NSA_QB
    nblk = S // SEL_BLOCK
    ncmp = kcmp.shape[1]
    t_cmp, t_near, t_win = tables
    cmpa = (CMP_STRIDE * np.arange(ncmp)[:, None] + (CMP_LEN - 1) - np.arange(NSA_QB)[None, :]).astype(np.int32)
    cmpa = jnp.asarray(np.tile(cmpa, (1, 4)))
    c = np.arange(ncmp)[None, :]
    j = np.arange(nblk)[:, None]
    ovl = ((CMP_STRIDE * c < (j + 1) * SEL_BLOCK) & (CMP_STRIDE * c + CMP_LEN > j * SEL_BLOCK)
           & (c < (S - CMP_LEN) // CMP_STRIDE + 1))
    ovl = jnp.asarray(ovl.astype(np.float32), dtype=_MM)
    sp = S + WINDOW
    const = lambda shape: pl.BlockSpec(shape, lambda b, i: (0,) * len(shape))
    return pl.pallas_call(
        _nsa_kernel,
        grid=(B, n_steps),
        in_specs=[
            pl.BlockSpec((None, NSA_QB, 512), lambda b, i: (b, i, Z_Q // 512)),
            pl.BlockSpec((None, NSA_QB, 128), lambda b, i: (b, i, Z_SMALL // 128)),
            pl.BlockSpec((None, ncmp, 128), lambda b, i: (b, 0, 0)),
            pl.BlockSpec((None, ncmp, 128), lambda b, i: (b, 0, 0)),
            pl.BlockSpec((None, sp, 128), lambda b, i: (b, 0, 0)),
            pl.BlockSpec((None, sp, 128), lambda b, i: (b, 0, 1)),
            pl.BlockSpec((None, sp, 128), lambda b, i: (b, 0, 2)),
            pl.BlockSpec((None, sp, 128), lambda b, i: (b, 0, 3)),
            const((ncmp, 512)), const((24, 1024)), const((256, 1024)),
            pl.BlockSpec((None, WINDOW + NSA_QB, 1024), lambda b, i: (jnp.minimum(i, 4), 0, 0)),
            const((nblk, ncmp)),
        ],
        out_specs=pl.BlockSpec((None, NSA_QB, 512), lambda b, i: (b, i, 0)),
        out_shape=jax.ShapeDtypeStruct((B, S, 512), _ACT),
        scratch_shapes=[pltpu.VMEM((16 + ncmp, 512), F32), pltpu.VMEM((2, nblk, NSA_QB), F32),
                        pltpu.VMEM((2, 8 + nblk, NSA_QB), F32), pltpu.VMEM((2, 8 + nblk, NSA_QB), F32)],
        compiler_params=_cp("parallel", "arbitrary"),
        name="nsa_attention",
    )(z, z, kcmp, vcmp, kv4, kv4, kv4, kv4, cmpa, t_cmp, t_near.astype(_SM), t_win.astype(_SM), ovl)


CONV_T = 256
CONV_SUB = 64
CONV_HALO = 32


def _conv_kernel(z_ref, w_ref, b_ref, g_ref, beta_ref, o_ref, ubuf):
    t = pl.program_id(1)

    @pl.when(t == 0)
    def _():
        ubuf[0:CONV_HALO, :] = jnp.zeros((CONV_HALO, CONV_WIDTH), F32)
        ubuf[CONV_HALO + CONV_T:, :] = jnp.zeros((8, CONV_WIDTH), F32)

    zt = z_ref[...].astype(F32)
    ubuf[CONV_HALO:CONV_HALO + CONV_T, :] = zt[:, :CONV_WIDTH] * _sigmoid(zt[:, CONV_WIDTH:])
    for r in range(0, CONV_T, CONV_SUB):
        acc = None
        for s in range(8):
            p = None
            for a in range((CONV_K + 1) // 8 + 1):
                k = 8 * a + s - (CONV_HALO - CONV_K + 1)
                if 0 <= k < CONV_K:
                    term = w_ref[k:k + 1, :] * ubuf[r + 8 * a:r + 8 * a + CONV_SUB + 8, :]
                    p = term if p is None else p + term
            acc = p[s:s + CONV_SUB] if acc is None else acc + p[s:s + CONV_SUB]
        y = _layer_norm(acc + b_ref[...], g_ref[...], beta_ref[...])
        o_ref[r:r + CONV_SUB, :] = (y * _sigmoid(y)).astype(o_ref.dtype)
    ubuf[0:CONV_HALO, :] = ubuf[CONV_T:CONV_T + CONV_HALO, :]


def _conformer_conv(z, conv_w, conv_b, conv_g, conv_beta):
    B, S, _ = z.shape
    w = jnp.concatenate([conv_w, jnp.zeros((1, CONV_WIDTH), conv_w.dtype)], axis=0).astype(F32)
    row = lambda a: a.reshape(1, CONV_WIDTH).astype(F32)
    const = lambda shape: pl.BlockSpec(shape, lambda b, t: (0, 0))
    return pl.pallas_call(
        _conv_kernel,
        grid=(B, S // CONV_T),
        in_specs=[pl.BlockSpec((None, CONV_T, 1024), lambda b, t: (b, t, Z_CONV // 1024)),
                  const((CONV_K + 1, CONV_WIDTH)), const((1, CONV_WIDTH)), const((1, CONV_WIDTH)),
                  const((1, CONV_WIDTH))],
        out_specs=pl.BlockSpec((None, CONV_T, CONV_WIDTH), lambda b, t: (b, t, 0)),
        out_shape=jax.ShapeDtypeStruct((B, S, CONV_WIDTH), _ACT),
        scratch_shapes=[pltpu.VMEM((CONV_HALO + CONV_T + 8, CONV_WIDTH), F32)],
        compiler_params=_cp("parallel", "arbitrary"),
        name="conformer_conv",
    )(z, w, row(conv_b), row(conv_g), row(conv_beta))


GLA_T = 256
_GLA_LEVELS = (1, 2, 4, 8, 16, 32)


def _gla_masks():
    t = np.arange(GLA_CHUNK)[:, None]
    s = np.arange(GLA_CHUNK)[None, :]
    ms = [(t == s)]
    for c in _GLA_LEVELS:
        ms.append(((t // c) % 2 == 1) & (s // c == t // c - 1))
    m = np.stack(ms).astype(np.float32)
    return np.tile(m, (1, GLA_HEADS, 1))


def _gla_kernel(gq_ref, gk_ref, gv_ref, sm_ref, gr_ref, gw_ref, gb_ref, gg_ref, tril_ref, lm_ref, hm_ref,
                o_ref, state):
    @pl.when(pl.program_id(1) == 0)
    def _():
        state[...] = jnp.zeros(state.shape, F32)

    C = GLA_CHUNK
    tidx = lax.broadcasted_iota(jnp.int32, (C, GLA_HEADS * GLA_DK), 0)

    def stack_heads(x):
        return jnp.concatenate([x * hm_ref[h:h + 1, :] for h in range(GLA_HEADS)], axis=0)

    def chunk_row(bb, r0):
        q = gq_ref[bb, pl.ds(r0, C), :].astype(F32) * (GLA_DK ** -0.5)
        k = gk_ref[bb, pl.ds(r0, C), :].astype(F32)
        v = gv_ref[bb, pl.ds(r0, C), :]
        pre = _dot(sm_ref[bb, pl.ds(r0, C), :], gw_ref[...]) + gb_ref[...]
        la = (jnp.minimum(pre, 0.0) - jnp.log1p(jnp.exp(-jnp.abs(pre)))) * (1.0 / GLA_TAU)
        b = _dot01_left(tril_ref[...], la)
        attn = _dot_nt(stack_heads(q), k) * lm_ref[0]
        bstart = b
        bnext = pltpu.roll(b, C - 1, axis=0)
        for li, c in enumerate(_GLA_LEVELS):
            odd = (tidx // c) % 2 == 1
            q_l = jnp.where(odd, q * jnp.exp(b - bstart), 0.0)
            k_l = jnp.where(odd, 0.0, k * jnp.exp(bnext - b))
            attn = attn + _dot_nt(stack_heads(q_l), k_l) * lm_ref[li + 1]
            half = (tidx % (2 * c)) < c
            bstart = jnp.where(half, bstart, pltpu.roll(bstart, c, axis=0))
            bnext = jnp.where(half, pltpu.roll(bnext, C - c, axis=0), bnext)
        st = state[bb]
        r_intra = _dot(attn, v)
        r_inter = _dot_nt(stack_heads(q * jnp.exp(b)), st)
        b_last = b[C - 1:C, :]
        upd = _dot_tn(v, k * jnp.exp(b_last - b))
        new_st = st * jnp.exp(b_last) + jnp.concatenate(
            [upd[128 * h:128 * h + 128, 64 * h:64 * h + 64] for h in range(GLA_HEADS)], axis=1)
        state[bb] = new_st
        outs = []
        for h in range(GLA_HEADS):
            o = r_intra[64 * h:64 * h + 64, 128 * h:128 * h + 128] + r_inter[64 * h:64 * h + 64, :]
            ms = jnp.mean(o * o, axis=-1, keepdims=True)
            outs.append(o * lax.rsqrt(ms + 1e-6) * gg_ref[...])
        gr = gr_ref[bb, pl.ds(r0, C), :].astype(F32)
        o_ref[bb, pl.ds(r0, C), :] = (jnp.concatenate(outs, axis=1) * (gr * _sigmoid(gr))).astype(o_ref.dtype)

    def chunk(ci, carry):
        r0 = pl.multiple_of(ci * C, C)
        for bb in range(gq_ref.shape[0]):
            chunk_row(bb, r0)
        return carry

    lax.fori_loop(0, GLA_T // C, chunk, 0)


def _gla(z, gla_w, gla_b, gla_g):
    B, S, _ = z.shape
    gw = jnp.zeros((128, GLA_HEADS * GLA_DK), F32).at[SMALL_GA:SMALL_GA + GLA_RANK].set(gla_w).astype(_MM)
    tril = jnp.asarray(np.tril(np.ones((GLA_CHUNK, GLA_CHUNK), np.float32)), dtype=_MM)
    lm = jnp.asarray(_gla_masks())
    hm = jnp.asarray(np.repeat(np.eye(GLA_HEADS, dtype=np.float32), GLA_DK, axis=1))
    const = lambda shape: pl.BlockSpec(shape, lambda b, t: (0,) * len(shape))
    nb = 2 if B % 2 == 0 else 1
    zspec = lambda w, off: pl.BlockSpec((nb, GLA_T, w), lambda b, t: (b, t, off // w))
    return pl.pallas_call(
        _gla_kernel,
        grid=(B // nb, S // GLA_T),
        in_specs=[zspec(256, Z_GQ), zspec(256, Z_GK), zspec(512, Z_GV), zspec(128, Z_SMALL), zspec(512, Z_GR),
                  const((128, 256)), const((1, 256)), const((1, GLA_DV)), const((GLA_CHUNK, GLA_CHUNK)),
                  const((7, GLA_HEADS * GLA_CHUNK, GLA_CHUNK)), const((GLA_HEADS, 256))],
        out_specs=pl.BlockSpec((nb, GLA_T, 512), lambda b, t: (b, t, 0)),
        out_shape=jax.ShapeDtypeStruct((B, S, 512), _ACT),
        scratch_shapes=[pltpu.VMEM((nb, GLA_DV, GLA_HEADS * GLA_DK), F32)],
        compiler_params=_cp("parallel", "arbitrary"),
        name="gla",
    )(z, z, z, z, z, gw, gla_b.reshape(1, -1).astype(F32), gla_g.reshape(1, -1).astype(F32), tril, lm, hm)


MERGE_T = 512


def _merge_kernel(oa_ref, ob_ref, oc_ref, mg_ref, wb_ref, wo_ref, x_ref, g_ref, b_ref, o_ref):
    merged = None
    for j, r in enumerate((oa_ref, ob_ref, oc_ref)):
        gate = _sigmoid(mg_ref[:, D_MODEL * j:D_MODEL * (j + 1)].astype(F32))
        term = gate * _dot(r[...], wb_ref[j])
        merged = term if merged is None else merged + term
    mix = _dot(merged, wo_ref[...])
    o_ref[...] = _layer_norm(DEEPNORM_ALPHA * x_ref[...] + mix, g_ref[...], b_ref[...])


def _merge(oa, ob, oc, z2, w_branch, w_out, x2, g, b):
    n = x2.shape[0]
    tok = lambda w, blk=0: pl.BlockSpec((MERGE_T, w), lambda i: (i, blk))
    const = lambda shape: pl.BlockSpec(shape, lambda i: (0,) * len(shape))
    return pl.pallas_call(
        _merge_kernel,
        grid=(n // MERGE_T,),
        in_specs=[tok(512), tok(512), tok(512), tok(3072, Z_MERGE // 3072), const((3, 512, D_MODEL)),
                  const((D_MODEL, D_MODEL)), tok(D_MODEL), const((1, D_MODEL)), const((1, D_MODEL))],
        out_specs=tok(D_MODEL),
        out_shape=jax.ShapeDtypeStruct((n, D_MODEL), F32),
        compiler_params=_cp("parallel"),
        name="merge_outproj_ln",
    )(oa, ob, oc, z2, w_branch.astype(_MM), w_out.astype(_MM), x2, g.reshape(1, -1), b.reshape(1, -1))


XA_T = 512


ROW_TILE = 8


def _store_row_tiles(ref, y):
    n = y.shape[0]
    for s in range(ROW_TILE):
        ref[pl.ds(s, n, stride=ROW_TILE), :] = y[:, 128 * s:128 * (s + 1)]


def _load_row_tiles(ref):
    n = ref.shape[0] // ROW_TILE
    return jnp.concatenate([ref[pl.ds(s, n, stride=ROW_TILE), :] for s in range(ROW_TILE)], axis=1)


def _xattn_kernel(x_ref, kv_ref, wq_ref, wo_ref, g_ref, b_ref, o_ref, o3_ref):
    x = x_ref[...]
    q = (_dot(x, wq_ref[...]) * (XA_DH ** -0.5)).astype(_MM)
    heads = []
    for h in range(XA_HEADS):
        hs = slice(XA_DH * h, XA_DH * (h + 1))
        s = _dot_nt(q[:, hs], kv_ref[:, hs])
        e = jnp.exp(s - jnp.max(s, axis=-1, keepdims=True))
        p = e / jnp.sum(e, axis=-1, keepdims=True)
        heads.append(_dot(p, kv_ref[:, D_MODEL + XA_DH * h:D_MODEL + XA_DH * (h + 1)]))
    xa = _dot(jnp.concatenate(heads, axis=1), wo_ref[...])
    y = _layer_norm(DEEPNORM_ALPHA * x + xa, g_ref[...], b_ref[...])
    o_ref[...] = y
    _store_row_tiles(o3_ref, y)


def _cross_attention(x, kv, wq, wo, g, b):
    B, S, _ = x.shape
    M = kv.shape[1]
    nt = S // XA_T
    const = lambda shape: pl.BlockSpec(shape, lambda bb, t: (0,) * len(shape))
    return pl.pallas_call(
        _xattn_kernel,
        grid=(B, nt),
        in_specs=[pl.BlockSpec((None, XA_T, D_MODEL), lambda bb, t: (bb, t, 0)),
                  pl.BlockSpec((None, M, 2 * D_MODEL), lambda bb, t: (bb, 0, 0)),
                  const((D_MODEL, D_MODEL)), const((D_MODEL, D_MODEL)), const((1, D_MODEL)), const((1, D_MODEL))],
        out_specs=[pl.BlockSpec((None, XA_T, D_MODEL), lambda bb, t: (bb, t, 0)),
                   pl.BlockSpec((XA_T * ROW_TILE, 128), lambda bb, t: (bb * nt + t, 0))],
        out_shape=[jax.ShapeDtypeStruct((B, S, D_MODEL), F32), jax.ShapeDtypeStruct((B * S * ROW_TILE, 128), F32)],
        compiler_params=_cp("parallel", "parallel"),
        name="cross_attention_ln",
    )(x, kv, wq.astype(_MM), wo.astype(_MM), g.reshape(1, -1), b.reshape(1, -1))


ROUTE_T = 512


def _router_kernel(x_ref, wt_ref, b_ref, upper_ref, ti_ref, gate_ref, rank_ref, cnt_ref, carry):
    @pl.when(pl.program_id(0) == 0)
    def _():
        carry[...] = jnp.zeros(carry.shape, F32)

    x = x_ref[...]
    xh = x.astype(_MM)
    xl = (x - xh.astype(F32)).astype(_MM)
    w = wt_ref[...]
    wh = w.astype(_MM)
    wl = (w - wh.astype(F32)).astype(_MM)
    logits = _dot_nt(wh, xh) + _dot_nt(wh, xl) + _dot_nt(wl, xh) + b_ref[...]
    eidx = lax.broadcasted_iota(jnp.int32, logits.shape, 0)
    v = logits
    tops, idxs, hots = [], [], []
    for _ in range(TOP_K):
        m = jnp.max(v, axis=0, keepdims=True)
        idx = jnp.min(jnp.where(v == m, eidx, N_EXPERTS), axis=0, keepdims=True)
        hot = eidx == idx
        v = jnp.where(hot, -jnp.inf, v)
        tops.append(m)
        idxs.append(idx)
        hots.append(jnp.where(hot, 1.0, 0.0))
    es = [jnp.exp(t - tops[0]) for t in tops]
    den = es[0] + es[1] + es[2] + es[3]
    multihot = hots[0] + hots[1] + hots[2] + hots[3]
    before = jnp.dot(multihot.astype(_MM), upper_ref[...], preferred_element_type=F32) + carry[...][:, 0:1]
    ranks = [jnp.sum(h * before, axis=0, keepdims=True) for h in hots]
    pad = jnp.zeros((8 - TOP_K, x.shape[0]), F32)
    ti_ref[...] = jnp.concatenate(idxs + [pad.astype(jnp.int32)], axis=0)
    gate_ref[...] = jnp.concatenate([e / den for e in es] + [pad], axis=0)
    rank_ref[...] = jnp.concatenate(ranks + [pad], axis=0).astype(jnp.int32)
    carry[...] = carry[...] + jnp.sum(multihot, axis=1, keepdims=True)
    cnt_ref[...] = carry[...]


def _router(x2, router_w, router_b):
    n = x2.shape[0]
    upper = jnp.asarray(np.triu(np.ones((ROUTE_T, ROUTE_T), np.float32), 1), dtype=_MM)
    bcol = jnp.broadcast_to(router_b.astype(F32)[:, None], (N_EXPERTS, ROUTE_T))
    tokspec = pl.BlockSpec((8, ROUTE_T), lambda i: (0, i))
    const = lambda shape: pl.BlockSpec(shape, lambda i: (0,) * len(shape))
    return pl.pallas_call(
        _router_kernel,
        grid=(n // ROUTE_T,),
        in_specs=[pl.BlockSpec((ROUTE_T, D_MODEL), lambda i: (i, 0)), const((N_EXPERTS, D_MODEL)),
                  const((N_EXPERTS, ROUTE_T)), const((ROUTE_T, ROUTE_T))],
        out_specs=[tokspec, tokspec, tokspec, const((N_EXPERTS, 128))],
        out_shape=[jax.ShapeDtypeStruct((8, n), jnp.int32), jax.ShapeDtypeStruct((8, n), F32),
                   jax.ShapeDtypeStruct((8, n), jnp.int32), jax.ShapeDtypeStruct((N_EXPERTS, 128), F32)],
        scratch_shapes=[pltpu.VMEM((N_EXPERTS, 128), F32)],
        compiler_params=_cp("arbitrary"),
        name="moe_router",
    )(x2, router_w.T.astype(F32), bcol, upper)


MOE_SLOTS = 3


def _moe_kernel(blk_e_ref, nvalid_ref, tok0_ref, tok1_ref, tok2_ref, dstp_ref, x_hbm, wgu_ref, bgu_ref, wdn_ref,
                bdn_ref, y_hbm, xbuf, ybuf, wgu_mm, wdn_mm, sem_in, sem_out):
    b = pl.program_id(0)
    nvalid = nvalid_ref[0]
    slot = b % MOE_SLOTS
    nxt = (b + 2) % MOE_SLOTS

    def gather_row(tok_ref, s, r, priority=0):
        src = x_hbm.at[pl.ds(pl.multiple_of(tok_ref[0, r], ROW_TILE), ROW_TILE), :]
        pltpu.make_async_copy(src, xbuf.at[s, pl.ds(ROW_TILE * r, ROW_TILE), :], sem_in.at[s]).start(priority)

    def scatter_row(s, r, priority=0):
        dst = y_hbm.at[pl.ds(pl.multiple_of(dstp_ref[0, r], ROW_TILE), ROW_TILE), :]
        pltpu.make_async_copy(ybuf.at[s, pl.ds(ROW_TILE * r, ROW_TILE), :], dst, sem_out.at[s]).start(priority)

    def wait_in(s):
        pltpu.make_async_copy(x_hbm.at[pl.ds(0, MOE_BLK * ROW_TILE), :], xbuf.at[s], sem_in.at[s]).wait()

    def wait_out(s):
        pltpu.make_async_copy(ybuf.at[s], y_hbm.at[pl.ds(0, MOE_BLK * ROW_TILE), :], sem_out.at[s]).wait()

    @pl.when(b == 0)
    def _():
        ybuf[...] = jnp.zeros(ybuf.shape, F32)

        def body(r, c):
            gather_row(tok0_ref, 0, r)
            gather_row(tok1_ref, 1, r)
            return c
        lax.fori_loop(0, MOE_BLK, body, 0)

    @pl.when(jnp.logical_and(b >= 2, b < nvalid))
    def _():
        wait_out(slot)

    @pl.when(jnp.logical_and(b < nvalid, jnp.logical_or(b == 0, blk_e_ref[b] != blk_e_ref[jnp.maximum(b - 1, 0)])))
    def _():
        wgu_mm[...] = wgu_ref[...].astype(_MM)
        wdn_mm[...] = wdn_ref[...].astype(_MM)

    @pl.when(b < nvalid)
    def _():
        wait_in(slot)
        xb = _load_row_tiles(xbuf.at[slot]).astype(_MM)
        for r in range(MOE_BLK):
            gather_row(tok2_ref, nxt, r, r % 2)
        gu = jnp.dot(xb, wgu_mm[...], preferred_element_type=F32) + bgu_ref[...]
        for r in range(MOE_BLK):
            scatter_row(nxt, r, r % 2)
        g = jnp.minimum(gu[:, :D_MODEL], SWIGLU_LIMIT)
        u = jnp.clip(gu[:, D_MODEL:], -SWIGLU_LIMIT, SWIGLU_LIMIT)
        h = (u + 1.0) * g * _sigmoid(SWIGLU_ALPHA * g)
        _store_row_tiles(ybuf.at[slot], jnp.dot(h.astype(_MM), wdn_mm[...], preferred_element_type=F32) + bdn_ref[...])

    @pl.when(b == nvalid)
    def _():
        def body(r, c):
            scatter_row(nxt, r)
            return c
        lax.fori_loop(0, MOE_BLK, body, 0)
        wait_out(nxt)
        wait_out((b + 1) % MOE_SLOTS)

        @pl.when(nvalid >= 2)
        def _():
            wait_out(slot)
        wait_in(slot)
        wait_in((b + 1) % MOE_SLOTS)


def _moe_experts(x3, slot_tok, slot_dst, blk_e, nvalid, n_out_rows, layer, w_gu, b_gu, w_dn, b_dn):
    n_blocks = blk_e.shape[0]
    tok3 = (slot_tok * ROW_TILE).reshape(n_blocks, 1, MOE_BLK)
    dst3 = (slot_dst * ROW_TILE).reshape(n_blocks + 1, 1, MOE_BLK)
    smem = lambda f: pl.BlockSpec((None, 1, MOE_BLK), f, memory_space=pltpu.SMEM)
    wspec = lambda r, c: pl.BlockSpec((None, None, r, c), lambda b, be, nv: (layer, be[b], 0, 0))
    grid_spec = pltpu.PrefetchScalarGridSpec(
        num_scalar_prefetch=2,
        grid=(n_blocks,),
        in_specs=[
            smem(lambda b, be, nv: (b, 0, 0)),
            smem(lambda b, be, nv: (jnp.minimum(b + 1, n_blocks - 1), 0, 0)),
            smem(lambda b, be, nv: (jnp.minimum(b + 2, n_blocks - 1), 0, 0)),
            smem(lambda b, be, nv: (b, 0, 0)),
            pl.BlockSpec(memory_space=pl.ANY),
            wspec(D_MODEL, 2 * D_MODEL), wspec(1, 2 * D_MODEL), wspec(D_MODEL, D_MODEL), wspec(1, D_MODEL),
        ],
        out_specs=pl.BlockSpec(memory_space=pl.ANY),
        scratch_shapes=[pltpu.VMEM((MOE_SLOTS, MOE_BLK * ROW_TILE, 128), F32),
                        pltpu.VMEM((MOE_SLOTS, MOE_BLK * ROW_TILE, 128), F32),
                        pltpu.VMEM((D_MODEL, 2 * D_MODEL), _MM), pltpu.VMEM((D_MODEL, D_MODEL), _MM),
                        pltpu.SemaphoreType.DMA((MOE_SLOTS,)), pltpu.SemaphoreType.DMA((MOE_SLOTS,))],
    )
    return pl.pallas_call(
        _moe_kernel,
        grid_spec=grid_spec,
        out_shape=jax.ShapeDtypeStruct((n_out_rows * ROW_TILE, 128), F32),
        compiler_params=_cp("arbitrary"),
        name="moe_experts",
    )(blk_e, nvalid, tok3, tok3, tok3, dst3, x3, w_gu, b_gu[:, :, None, :], w_dn, b_dn[:, :, None, :])


def _combine_kernel(y0_ref, y1_ref, y2_ref, y3_ref, gate_ref, x_ref, g_ref, b_ref, o_ref):
    gate = gate_ref[...]
    ff = gate[:, 0:1] * _load_row_tiles(y0_ref)
    for k, r in enumerate((y1_ref, y2_ref, y3_ref)):
        ff = ff + gate[:, k + 1:k + 2] * _load_row_tiles(r)
    o_ref[...] = _layer_norm(DEEPNORM_ALPHA * x_ref[...] + ff, g_ref[...], b_ref[...])


def _moe_combine(ys, gate_t, x2, g, b):
    n = x2.shape[0]
    steps = n // COMB_T
    const = lambda shape: pl.BlockSpec(shape, lambda i: (0,) * len(shape))
    yspec = lambda k: pl.BlockSpec((COMB_T * ROW_TILE, 128), lambda i: (k * steps + i, 0))
    return pl.pallas_call(
        _combine_kernel,
        grid=(steps,),
        in_specs=[yspec(0), yspec(1), yspec(2), yspec(3), pl.BlockSpec((COMB_T, 8), lambda i: (i, 0)),
                  pl.BlockSpec((COMB_T, D_MODEL), lambda i: (i, 0)), const((1, D_MODEL)), const((1, D_MODEL))],
        out_specs=pl.BlockSpec((COMB_T, D_MODEL), lambda i: (i, 0)),
        out_shape=jax.ShapeDtypeStruct((n, D_MODEL), F32),
        compiler_params=_cp("parallel"),
        name="moe_combine_ln",
    )(ys, ys, ys, ys, gate_t, x2, g.reshape(1, -1), b.reshape(1, -1))


def _moe_ffn(x2, x3, router_w, router_b, layer, w_gu, b_gu, w_dn, b_dn, g, b):
    n = x2.shape[0]
    a = n * TOP_K
    top_i, gate, rank, cnt = _router(x2, router_w, router_b)
    counts = cnt[:, 0].astype(jnp.int32)
    padded = (counts + MOE_BLK - 1) // MOE_BLK * MOE_BLK
    pends = jnp.cumsum(padded)
    pstarts = pends - padded
    n_blocks = a // MOE_BLK + N_EXPERTS
    n_slots = n_blocks * MOE_BLK
    eids = jnp.arange(N_EXPERTS, dtype=jnp.int32)
    hot = top_i[:TOP_K, None, :] == eids[None, :, None]
    dest = jnp.sum(jnp.where(hot, pstarts[None, :, None], 0), axis=1) + rank[:TOP_K]
    spare = a + jnp.arange(n_slots, dtype=jnp.int32) % MOE_BLK
    slot_asg = spare.at[dest.reshape(-1)].set(jnp.arange(a, dtype=jnp.int32))
    slot_tok = jnp.where(slot_asg < a, slot_asg % n, 0)
    slot_dst = jnp.concatenate([spare[:MOE_BLK], slot_asg])
    blk_start = jnp.arange(n_blocks, dtype=jnp.int32) * MOE_BLK
    blk_e = jnp.minimum(jnp.sum((pends[None, :] <= blk_start[:, None]).astype(jnp.int32), axis=1), N_EXPERTS - 1)
    nvalid = (pends[-1:] // MOE_BLK).astype(jnp.int32)
    ys = _moe_experts(x3, slot_tok, slot_dst, blk_e, nvalid, a + MOE_BLK, layer, w_gu, b_gu, w_dn, b_dn)
    return _moe_combine(ys, gate.T, x2, g, b)


def _layer(x, mem, tables, w_in, cmp_pe, cmp_w1, cmp_b1, cmp_w2, conv_w, conv_b, conv_g, conv_beta, gla_w, gla_b,
           gla_g, w_branch, w_out, xa_wq, xa_wkv, xa_wo, router_w, router_b, layer, w_gu, b_gu, w_dn, b_dn, ng, nb):
    B, S, D = x.shape
    n = B * S
    x2 = x.reshape(n, D)
    z2 = _matmul(x2, _reorder_w_in(w_in), 1024, ZW // 5, _ACT)
    z = z2.reshape(B, S, ZW)
    kvc = z[:, :, Z_KV:Z_KV + 256].reshape(B, S // 16, 16, 2, 128).transpose(3, 0, 1, 2, 4)
    cmp = _compress(kvc.reshape(2, B, S // 16, 2048), cmp_pe, cmp_w1, cmp_b1, cmp_w2)
    kv4 = jnp.pad(z[:, :, Z_KV + 256:Z_KV + 768], ((0, 0), (WINDOW, 0), (0, 0)))
    oa = _nsa_attention(z, cmp[0], cmp[1], kv4, tables)
    ob = _conformer_conv(z, conv_w, conv_b, conv_g, conv_beta)
    oc = _gla(z, gla_w, gla_b, gla_g)
    x2 = _merge(oa.reshape(n, 512), ob.reshape(n, 512), oc.reshape(n, 512), z2, w_branch, w_out, x2, ng[0], nb[0])
    kv = _matmul(mem.reshape(-1, D), xa_wkv.astype(_MM), 1024, 1024, _ACT).reshape(B, -1, 2 * D)
    x2, x3 = _cross_attention(x2.reshape(B, S, D), kv, xa_wq, xa_wo, ng[1], nb[1])
    x2 = _moe_ffn(x2.reshape(n, D), x3, router_w, router_b, layer, w_gu, b_gu, w_dn, b_dn, ng[2], nb[2])
    return x2.reshape(B, S, D)


def kernel(x, mem, rel_bias, w_in, cmp_pe, cmp_w1, cmp_b1, cmp_w2, conv_w, conv_b, conv_norm_g, conv_norm_b,
           gla_gate_w, gla_gate_b, gla_norm_g, w_branch, w_out, xa_wq, xa_wkv, xa_wo, router_w, router_b,
           expert_w_gu, expert_b_gu, expert_w_down, expert_b_down, norm_g, norm_b):
    tables = _nsa_tables(rel_bias)
    for l in range(DEPTH):
        x = _layer(x, mem, tables, w_in[l], cmp_pe[l], cmp_w1[l], cmp_b1[l], cmp_w2[l], conv_w[l], conv_b[l],
                   conv_norm_g[l], conv_norm_b[l], gla_gate_w[l], gla_gate_b[l], gla_norm_g[l], w_branch[l],
                   w_out[l], xa_wq[l], xa_wkv[l], xa_wo[l], router_w[l], router_b[l], l, expert_w_gu,
                   expert_b_gu, expert_w_down, expert_b_down, norm_g[l], norm_b[l])
    return x
```

```python
import functools
import math

import numpy as np
import jax
import jax.numpy as jnp
from jax import lax
from jax.experimental import pallas as pl
from jax.experimental.pallas import tpu as pltpu

F32 = jnp.float32
_MM = jnp.bfloat16
_ACT = jnp.bfloat16
_SM = jnp.float32
NEG = -1e30
LOG2E = 1.4426950408889634

D_MODEL = 1024
DEPTH = 2
MEM_LEN = 256
NSA_HEADS = 8
NSA_DH = 64
CMP_LEN = 32
CMP_STRIDE = 16
SEL_BLOCK = 64
SEL_TOP = 16
WINDOW = 512
CONV_WIDTH = 512
CONV_K = 31
GLA_HEADS = 4
GLA_DK = 64
GLA_DV = 128
GLA_RANK = 16
GLA_TAU = 16.0
GLA_CHUNK = 64
REL_BUCKETS = 32
REL_MAX_DIST = 128
XA_HEADS = 4
XA_DH = 256
N_EXPERTS = 32
TOP_K = 4
SWIGLU_ALPHA = 1.702
SWIGLU_LIMIT = 7.0
DEEPNORM_ALPHA = (2 * DEPTH) ** 0.25

Z_MERGE = 0
Z_CONV = 3072
Z_Q = 4096
Z_GV = 4608
Z_GR = 5120
Z_KV = 5632
Z_GQ = 6400
Z_GK = 6656
Z_SMALL = 6912
ZW = 7040
SMALL_GA = 24

_IN_SIZES = (512, 768, 24, 1024, 256, 256, 512, 16, 512, 3072)
_IN_OFF = np.concatenate([[0], np.cumsum(_IN_SIZES)]).astype(int)

VMEM_LIMIT = 56 * 1024 * 1024

NSA_QB = 128
NSA_SUBS = 2
MOE_BLK = 512
COMB_T = 256


def _cp(*sem):
    return pltpu.CompilerParams(dimension_semantics=sem, vmem_limit_bytes=VMEM_LIMIT)


def _dot(a, b):
    return jnp.dot(a.astype(_MM), b.astype(_MM), preferred_element_type=F32)


def _dot_nt(a, b):
    return lax.dot_general(a.astype(_MM), b.astype(_MM), (((1,), (1,)), ((), ())), preferred_element_type=F32)


def _dot_tn(a, b):
    return lax.dot_general(a.astype(_MM), b.astype(_MM), (((0,), (0,)), ((), ())), preferred_element_type=F32)


def _split3(x):
    x1 = x.astype(_MM)
    r1 = x - x1.astype(F32)
    x2 = r1.astype(_MM)
    x3 = (r1 - x2.astype(F32)).astype(_MM)
    return x1, x2, x3


def _dot01_left(m01, x):
    x1, x2, x3 = _split3(x)
    return (jnp.dot(m01, x1, preferred_element_type=F32) + jnp.dot(m01, x2, preferred_element_type=F32)
            + jnp.dot(m01, x3, preferred_element_type=F32))


def _layer_norm(y, g, b, eps=1e-5):
    mu = jnp.mean(y, axis=-1, keepdims=True)
    d = y - mu
    var = jnp.mean(d * d, axis=-1, keepdims=True)
    return d * lax.rsqrt(var + eps) * g + b


def _sigmoid(x):
    return 1.0 / (1.0 + jnp.exp(-x))


def _mm_kernel(x_ref, w_ref, o_ref):
    o_ref[...] = _dot(x_ref[...], w_ref[...]).astype(o_ref.dtype)


def _matmul(x, w, tm, tn, out_dtype):
    m, k = x.shape
    n = w.shape[1]
    tm = min(tm, m)
    return pl.pallas_call(
        _mm_kernel,
        grid=(m // tm, n // tn),
        in_specs=[pl.BlockSpec((tm, k), lambda i, j: (i, 0)), pl.BlockSpec((k, tn), lambda i, j: (0, j))],
        out_specs=pl.BlockSpec((tm, tn), lambda i, j: (i, j)),
        out_shape=jax.ShapeDtypeStruct((m, n), out_dtype),
        compiler_params=_cp("parallel", "parallel"),
        name="matmul",
    )(x, w)


def _reorder_w_in(w_in):
    def cols(i):
        return w_in[:, _IN_OFF[i]:_IN_OFF[i + 1]]
    small = jnp.concatenate([cols(2), cols(7), jnp.zeros((D_MODEL, 128 - 40), w_in.dtype)], axis=1)
    w = jnp.concatenate([cols(9), cols(3), cols(0), cols(6), cols(8), cols(1), cols(4), cols(5), small], axis=1)
    return w.astype(_MM)


def _cmp_kernel(x_ref, pea_ref, peb_ref, wa_ref, wb_ref, b1_ref, w2_ref, o_ref):
    x = x_ref[...].astype(F32)
    a = _dot(x + pea_ref[...], wa_ref[...])
    bm = _dot(x + peb_ref[...], wb_ref[...])
    n = bm.shape[0]
    bs = pltpu.roll(bm, n - 1, axis=0)
    h = jax.nn.gelu(a + bs + b1_ref[...], approximate=True)
    o_ref[...] = _dot(h, w2_ref[...]).astype(o_ref.dtype)


def _compress(kv2, cmp_pe, cmp_w1, cmp_b1, cmp_w2):
    _, B, R, _ = kv2.shape
    eye2 = jnp.eye(2, dtype=F32)

    def half(w1h):
        w = jnp.einsum('rlde,gh->rlgdhe', w1h, eye2)
        return w.reshape(2, 16 * 128, 128).astype(_MM)

    wa = half(cmp_w1[:, :16])
    wb = half(cmp_w1[:, 16:])

    def pe_half(p):
        return jnp.broadcast_to(p[:, :, None, :], (2, 16, 2, 64)).reshape(2, 1, 2048).astype(F32)

    pea = pe_half(cmp_pe[:, :16])
    peb = pe_half(cmp_pe[:, 16:])
    b1 = jnp.tile(cmp_b1, (1, 2)).reshape(2, 1, 128).astype(F32)
    w2 = jnp.einsum('rde,gh->rgdhe', cmp_w2, eye2).reshape(2, 128, 128).astype(_MM)
    return pl.pallas_call(
        _cmp_kernel,
        grid=(2, B),
        in_specs=[
            pl.BlockSpec((None, None, R, 2048), lambda r, b: (r, b, 0, 0)),
            pl.BlockSpec((None, 1, 2048), lambda r, b: (r, 0, 0)),
            pl.BlockSpec((None, 1, 2048), lambda r, b: (r, 0, 0)),
            pl.BlockSpec((None, 2048, 128), lambda r, b: (r, 0, 0)),
            pl.BlockSpec((None, 2048, 128), lambda r, b: (r, 0, 0)),
            pl.BlockSpec((None, 1, 128), lambda r, b: (r, 0, 0)),
            pl.BlockSpec((None, 128, 128), lambda r, b: (r, 0, 0)),
        ],
        out_specs=pl.BlockSpec((None, None, R, 128), lambda r, b: (r, b, 0, 0)),
        out_shape=jax.ShapeDtypeStruct((2, B, R, 128), _ACT),
        compiler_params=_cp("parallel", "parallel"),
        name="nsa_compress",
    )(kv2, pea, peb, wa, wb, b1, w2)


def _t5_bucket_np(dist):
    n = np.maximum(dist, 0)
    exact = REL_BUCKETS // 2
    lr = np.log(np.maximum(n, 1).astype(np.float32) / np.float32(exact)) / np.float32(math.log(REL_MAX_DIST / exact))
    large = exact + (lr * np.float32(REL_BUCKETS - exact)).astype(np.int32)
    return np.where(n < exact, n, np.minimum(large, REL_BUCKETS - 1))


def _nsa_tables(rel_bias):
    rel = (rel_bias - rel_bias[REL_BUCKETS - 1:REL_BUCKETS]).astype(F32) * LOG2E
    q = np.arange(NSA_QB)

    def table(dist, valid, fill):
        hot = jnp.asarray(_t5_bucket_np(dist), jnp.int32)[:, :, None] == jnp.arange(REL_BUCKETS)[None, None, :]
        t = jnp.sum(jnp.where(hot[..., None], rel[None, None], 0.0), axis=2)
        t = jnp.where(valid[:, :, None], t, fill)
        return jnp.transpose(t, (0, 2, 1)).reshape(dist.shape[0], NSA_HEADS * NSA_QB)

    cc = np.arange(24) - 16
    d_c = q[None, :] - CMP_STRIDE * cc[:, None] - (CMP_LEN - 1)
    t_cmp = table(d_c, d_c >= 0, 0.0)
    ko = np.arange(256)
    d_n = q[None, :] + 128 - ko[:, None]
    t_near = table(d_n, d_n >= 0, NEG)
    kw = np.arange(WINDOW + NSA_QB)
    d_w = q[None, :] + WINDOW - kw[:, None]
    in_win = (d_w >= 0) & (d_w < WINDOW)
    t_win = jnp.stack([table(d_w, in_win & (kw[:, None] >= WINDOW - NSA_QB * v), NEG) for v in range(5)])
    return t_cmp, t_near, t_win


def _nsa_kernel(q_ref, gate_ref, kcmp_ref, vcmp_ref, ksel_ref, vsel_ref, kwin_ref, vwin_ref,
                cmpa_ref, tcmp_ref, tnear_ref, twin0_ref, twin1_ref, ovl_ref, out_ref, s_scr, v_scr, sel_scr, far_scr):
    step = pl.program_id(1)
    ncmp = kcmp_ref.shape[0]
    nblk = ovl_ref.shape[0]
    zeros64 = jnp.zeros((64, NSA_QB), F32)
    jidx = lax.broadcasted_iota(jnp.int32, (nblk, NSA_QB), 0)
    qidx = lax.broadcasted_iota(jnp.int32, (nblk, NSA_QB), 1)
    sub8 = lax.broadcasted_iota(jnp.int32, (8, NSA_QB), 0)
    neg8 = jnp.full((8, NSA_QB), NEG, F32)
    twin_refs = (twin0_ref, twin1_ref)

    def mask_rows(scr, first_block, n):
        rows = [jnp.broadcast_to(scr[pl.ds(8 + first_block + b, 1), :], (SEL_BLOCK, NSA_QB)) for b in range(n)]
        return jnp.concatenate(rows, axis=0)

    def add_mask(s, m):
        return jnp.concatenate([s[:, 128 * h:128 * h + 128] + m for h in range(4)], axis=1)

    def col_max(x):
        if x.dtype == F32:
            return jnp.max(x, axis=0, keepdims=True)
        y = jnp.max(x.reshape(x.shape[0] // 16, 16, x.shape[1]), axis=0)
        return jnp.max(y.astype(F32), axis=0, keepdims=True)

    def col_sum(e):
        if e.dtype == F32:
            return jnp.sum(e, axis=0, keepdims=True)
        return jnp.dot(jnp.ones((8, e.shape[0]), e.dtype), e, preferred_element_type=F32)[0:1]

    def probs(s, table, mask, m_old=None):
        x = s.astype(_SM)
        if table is not None:
            x = x + table
        if mask is not None:
            x = add_mask(x, mask.astype(_SM))
        m = col_max(x)
        if m_old is not None:
            m = jnp.maximum(m_old, m)
        e = jnp.exp2(x - m.astype(_SM))
        return m, e, col_sum(e)

    q_gs, o_cs, o_ws, gts, state = [], [], [], [], []
    for sub, g in [(sub, g) for sub in range(NSA_SUBS) for g in range(2)]:
        i = NSA_SUBS * step + sub
        qrows = slice(NSA_QB * sub, NSA_QB * (sub + 1))
        if g == 0:
            qt = (q_ref[qrows, :].astype(F32) * (NSA_DH ** -0.5 * LOG2E)).T
            gts.append(_sigmoid(gate_ref[qrows, :].astype(F32)).T)
            cur = 2 * i + (qidx >= SEL_BLOCK).astype(jnp.int32)
            forced = (jidx == 0) | (jidx == cur) | (jidx == cur - 1)
            future = jidx > cur
            s_scr[sub, 0:16, :] = jnp.zeros((16, 4 * NSA_QB), F32)
        twin_ref = twin_refs[sub]
        blocks = []
        for h in range(4):
            r = qt[64 * (4 * g + h):64 * (4 * g + h) + 64]
            blocks.append(jnp.concatenate([r, zeros64] if g == 0 else [zeros64, r], axis=0))
        q_g = jnp.concatenate(blocks, axis=1).astype(_MM)
        q_gs.append(q_g)
        cs = slice(512 * g, 512 * g + 512)

        s_scr[sub, 16:16 + ncmp, :] = _dot(kcmp_ref[...], q_g)
        w0 = pl.multiple_of(8 * i, 8)
        s_scr[sub, pl.ds(w0, 24), :] = s_scr[sub, pl.ds(w0, 24), :] + tcmp_ref[:, cs]
        sc = jnp.where(cmpa_ref[...] <= NSA_QB * i, s_scr[sub, 16:16 + ncmp, :], NEG)
        m = jnp.maximum(jnp.max(sc, axis=0, keepdims=True), 0.1 * NEG)
        e = jnp.exp2(sc - m)
        l = jnp.sum(e, axis=0, keepdims=True)
        pc = e / jnp.maximum(l, 1e-30)
        o_cs.append(_dot_tn(vcmp_ref[...], pc))

        psum = pc[:, 0:128] + pc[:, 128:256] + pc[:, 256:384] + pc[:, 384:512]
        imp = _dot01_left(ovl_ref[...], psum)
        v = jnp.where(forced, 1e30, jnp.where(future, -1.0, imp))
        v_scr[sub, g] = v
        sel_scr[sub, g, 0:8, :] = neg8
        far_scr[sub, g, 0:8, :] = neg8
        for r8 in range(0, nblk, 8):
            vr = v[r8:r8 + 8]
            cnt = jnp.zeros((8, NSA_QB), F32)
            for jp in range(nblk):
                row = jnp.broadcast_to(v_scr[sub, g, pl.ds(jp, 1), :], (8, NSA_QB))
                if jp < r8:
                    cnt = cnt + jnp.where(row >= vr, 1.0, 0.0)
                elif jp >= r8 + 8:
                    cnt = cnt + jnp.where(row > vr, 1.0, 0.0)
                else:
                    cnt = cnt + jnp.where(sub8 > jp - r8, jnp.where(row >= vr, 1.0, 0.0),
                                          jnp.where(row > vr, 1.0, 0.0))
            sel = jnp.where(cnt < float(min(SEL_TOP, nblk)), 0.0, NEG)
            sel_scr[sub, g, 8 + r8:16 + r8, :] = sel
            far_scr[sub, g, 8 + r8:16 + r8, :] = jnp.where(jidx[r8:r8 + 8] <= 2 * i - 3, sel, NEG)

        n0 = pl.multiple_of(NSA_QB * i + WINDOW - 128, 128)
        m_s, e, l_s = probs(_dot(ksel_ref[pl.ds(n0, 256), :], q_g), tnear_ref[:, cs],
                            mask_rows(sel_scr.at[sub, g], 2 * i - 2, 4))
        state += [m_s, l_s, _dot_tn(vsel_ref[pl.ds(n0, 256), :], e)]

        w_start = pl.multiple_of(NSA_QB * i, 128)
        _, e, l_w = probs(_dot(kwin_ref[pl.ds(w_start, WINDOW + NSA_QB), :], q_g), twin_ref[:, cs], None)
        o_ws.append(_dot_tn(vwin_ref[pl.ds(w_start, WINDOW + NSA_QB), :], e) / l_w)

    def far_chunk(c, carry):
        k0 = pl.multiple_of(WINDOW + 512 * c, 512)
        kf = ksel_ref[pl.ds(k0, 512), :]
        vf = vsel_ref[pl.ds(k0, 512), :]
        new = []
        for n in range(2 * NSA_SUBS):
            m_o, l_o, a_o = carry[3 * n:3 * n + 3]
            m_n, ef, l_c = probs(_dot(kf, q_gs[n]), None, mask_rows(far_scr.at[n // 2, n % 2], 8 * c, 8), m_o)
            alpha = jnp.exp2(m_o - m_n)
            new += [m_n, alpha * l_o + l_c, alpha * a_o + _dot_tn(vf, ef)]
        return tuple(new)

    i_last = NSA_SUBS * step + NSA_SUBS - 1
    state = lax.fori_loop(0, (2 * i_last + 5) // 8, far_chunk, tuple(state))

    for sub in range(NSA_SUBS):
        gt = gts[sub]
        out_rows = []
        for g in range(2):
            n = 2 * sub + g
            o_s = state[3 * n + 2] / state[3 * n + 1]
            for h in range(4):
                hs = slice(128 * h, 128 * h + 128)
                ds_ = slice(64 * g, 64 * g + 64)
                gi = (4 * g + h) * 3
                out_rows.append(gt[gi:gi + 1, :] * o_cs[n][ds_, hs] + gt[gi + 1:gi + 2, :] * o_s[ds_, hs]
                                + gt[gi + 2:gi + 3, :] * o_ws[n][ds_, hs])
        out_ref[NSA_QB * sub:NSA_QB * (sub + 1), :] = jnp.concatenate(out_rows, axis=0).T.astype(out_ref.dtype)


def _nsa_attention(z, kcmp, vcmp, kv4, tables):
    B, S, _ = z.shape
    qs = NSA_QB * NSA_SUBS
    n_steps = S // qs
    nblk = S // SEL_BLOCK
    ncmp = kcmp.shape[1]
    t_cmp, t_near, t_win = tables
    cmpa = (CMP_STRIDE * np.arange(ncmp)[:, None] + (CMP_LEN - 1) - np.arange(NSA_QB)[None, :]).astype(np.int32)
    cmpa = jnp.asarray(np.tile(cmpa, (1, 4)))
    c = np.arange(ncmp)[None, :]
    j = np.arange(nblk)[:, None]
    ovl = ((CMP_STRIDE * c < (j + 1) * SEL_BLOCK) & (CMP_STRIDE * c + CMP_LEN > j * SEL_BLOCK)
           & (c < (S - CMP_LEN) // CMP_STRIDE + 1))
    ovl = jnp.asarray(ovl.astype(np.float32), dtype=_MM)
    sp = S + WINDOW
    const = lambda shape: pl.BlockSpec(shape, lambda b, i: (0,) * len(shape))
    return pl.pallas_call(
        _nsa_kernel,
        grid=(B, n_steps),
        in_specs=[
            pl.BlockSpec((None, qs, 512), lambda b, i: (b, i, Z_Q // 512)),
            pl.BlockSpec((None, qs, 128), lambda b, i: (b, i, Z_SMALL // 128)),
            pl.BlockSpec((None, ncmp, 128), lambda b, i: (b, 0, 0)),
            pl.BlockSpec((None, ncmp, 128), lambda b, i: (b, 0, 0)),
            pl.BlockSpec((None, sp, 128), lambda b, i: (b, 0, 0)),
            pl.BlockSpec((None, sp, 128), lambda b, i: (b, 0, 1)),
            pl.BlockSpec((None, sp, 128), lambda b, i: (b, 0, 2)),
            pl.BlockSpec((None, sp, 128), lambda b, i: (b, 0, 3)),
            const((ncmp, 512)), const((24, 1024)), const((256, 1024)),
            pl.BlockSpec((None, WINDOW + NSA_QB, 1024), lambda b, i: (jnp.minimum(NSA_SUBS * i, 4), 0, 0)),
            pl.BlockSpec((None, WINDOW + NSA_QB, 1024), lambda b, i: (jnp.minimum(NSA_SUBS * i + 1, 4), 0, 0)),
            const((nblk, ncmp)),
        ],
        out_specs=pl.BlockSpec((None, qs, 512), lambda b, i: (b, i, 0)),
        out_shape=jax.ShapeDtypeStruct((B, S, 512), _ACT),
        scratch_shapes=[pltpu.VMEM((NSA_SUBS, 16 + ncmp, 512), F32), pltpu.VMEM((NSA_SUBS, 2, nblk, NSA_QB), F32),
                        pltpu.VMEM((NSA_SUBS, 2, 8 + nblk, NSA_QB), F32),
                        pltpu.VMEM((NSA_SUBS, 2, 8 + nblk, NSA_QB), F32)],
        compiler_params=_cp("parallel", "arbitrary"),
        name="nsa_attention",
    )(z, z, kcmp, vcmp, kv4, kv4, kv4, kv4, cmpa, t_cmp, t_near.astype(_SM), t_win.astype(_SM),
      t_win.astype(_SM), ovl)


CONV_T = 256
CONV_SUB = 64
CONV_HALO = 32


def _conv_kernel(z_ref, w_ref, b_ref, g_ref, beta_ref, o_ref, ubuf):
    t = pl.program_id(1)

    @pl.when(t == 0)
    def _():
        ubuf[0:CONV_HALO, :] = jnp.zeros((CONV_HALO, CONV_WIDTH), F32)
        ubuf[CONV_HALO + CONV_T:, :] = jnp.zeros((8, CONV_WIDTH), F32)

    zt = z_ref[...].astype(F32)
    ubuf[CONV_HALO:CONV_HALO + CONV_T, :] = zt[:, :CONV_WIDTH] * _sigmoid(zt[:, CONV_WIDTH:])
    for r in range(0, CONV_T, CONV_SUB):
        acc = None
        for s in range(8):
            p = None
            for a in range((CONV_K + 1) // 8 + 1):
                k = 8 * a + s - (CONV_HALO - CONV_K + 1)
                if 0 <= k < CONV_K:
                    term = w_ref[k:k + 1, :] * ubuf[r + 8 * a:r + 8 * a + CONV_SUB + 8, :]
                    p = term if p is None else p + term
            acc = p[s:s + CONV_SUB] if acc is None else acc + p[s:s + CONV_SUB]
        y = _layer_norm(acc + b_ref[...], g_ref[...], beta_ref[...])
        o_ref[r:r + CONV_SUB, :] = (y * _sigmoid(y)).astype(o_ref.dtype)
    ubuf[0:CONV_HALO, :] = ubuf[CONV_T:CONV_T + CONV_HALO, :]


def _conformer_conv(z, conv_w, conv_b, conv_g, conv_beta):
    B, S, _ = z.shape
    w = jnp.concatenate([conv_w, jnp.zeros((1, CONV_WIDTH), conv_w.dtype)], axis=0).astype(F32)
    row = lambda a: a.reshape(1, CONV_WIDTH).astype(F32)
    const = lambda shape: pl.BlockSpec(shape, lambda b, t: (0, 0))
    return pl.pallas_call(
        _conv_kernel,
        grid=(B, S // CONV_T),
        in_specs=[pl.BlockSpec((None, CONV_T, 1024), lambda b, t: (b, t, Z_CONV // 1024)),
                  const((CONV_K + 1, CONV_WIDTH)), const((1, CONV_WIDTH)), const((1, CONV_WIDTH)),
                  const((1, CONV_WIDTH))],
        out_specs=pl.BlockSpec((None, CONV_T, CONV_WIDTH), lambda b, t: (b, t, 0)),
        out_shape=jax.ShapeDtypeStruct((B, S, CONV_WIDTH), _ACT),
        scratch_shapes=[pltpu.VMEM((CONV_HALO + CONV_T + 8, CONV_WIDTH), F32)],
        compiler_params=_cp("parallel", "arbitrary"),
        name="conformer_conv",
    )(z, w, row(conv_b), row(conv_g), row(conv_beta))


GLA_T = 256
_GLA_LEVELS = (1, 2, 4, 8, 16, 32)


def _gla_masks():
    t = np.arange(GLA_CHUNK)[:, None]
    s = np.arange(GLA_CHUNK)[None, :]
    ms = [(t == s)]
    for c in _GLA_LEVELS:
        ms.append(((t // c) % 2 == 1) & (s // c == t // c - 1))
    m = np.stack(ms).astype(np.float32)
    return np.tile(m, (1, GLA_HEADS, 1))


def _gla_kernel(gq_ref, gk_ref, gv_ref, sm_ref, gr_ref, gw_ref, gb_ref, gg_ref, tril_ref, lm_ref, hm_ref,
                o_ref, state):
    @pl.when(pl.program_id(1) == 0)
    def _():
        state[...] = jnp.zeros(state.shape, F32)

    C = GLA_CHUNK
    tidx = lax.broadcasted_iota(jnp.int32, (C, GLA_HEADS * GLA_DK), 0)

    def stack_heads(x):
        return jnp.concatenate([x * hm_ref[h:h + 1, :] for h in range(GLA_HEADS)], axis=0)

    def chunk_row(bb, r0):
        q = gq_ref[bb, pl.ds(r0, C), :].astype(F32) * (GLA_DK ** -0.5)
        k = gk_ref[bb, pl.ds(r0, C), :].astype(F32)
        v = gv_ref[bb, pl.ds(r0, C), :]
        pre = _dot(sm_ref[bb, pl.ds(r0, C), :], gw_ref[...]) + gb_ref[...]
        la = (jnp.minimum(pre, 0.0) - jnp.log1p(jnp.exp(-jnp.abs(pre)))) * (1.0 / GLA_TAU)
        b = _dot01_left(tril_ref[...], la)
        attn = _dot_nt(stack_heads(q), k) * lm_ref[0]
        bstart = b
        bnext = pltpu.roll(b, C - 1, axis=0)
        for li, c in enumerate(_GLA_LEVELS):
            odd = (tidx // c) % 2 == 1
            q_l = jnp.where(odd, q * jnp.exp(b - bstart), 0.0)
            k_l = jnp.where(odd, 0.0, k * jnp.exp(bnext - b))
            attn = attn + _dot_nt(stack_heads(q_l), k_l) * lm_ref[li + 1]
            half = (tidx % (2 * c)) < c
            bstart = jnp.where(half, bstart, pltpu.roll(bstart, c, axis=0))
            bnext = jnp.where(half, pltpu.roll(bnext, C - c, axis=0), bnext)
        st = state[bb]
        r_intra = _dot(attn, v)
        r_inter = _dot_nt(stack_heads(q * jnp.exp(b)), st)
        b_last = b[C - 1:C, :]
        upd = _dot_tn(v, k * jnp.exp(b_last - b))
        new_st = st * jnp.exp(b_last) + jnp.concatenate(
            [upd[128 * h:128 * h + 128, 64 * h:64 * h + 64] for h in range(GLA_HEADS)], axis=1)
        state[bb] = new_st
        outs = []
        for h in range(GLA_HEADS):
            o = r_intra[64 * h:64 * h + 64, 128 * h:128 * h + 128] + r_inter[64 * h:64 * h + 64, :]
            ms = jnp.mean(o * o, axis=-1, keepdims=True)
            outs.append(o * lax.rsqrt(ms + 1e-6) * gg_ref[...])
        gr = gr_ref[bb, pl.ds(r0, C), :].astype(F32)
        o_ref[bb, pl.ds(r0, C), :] = (jnp.concatenate(outs, axis=1) * (gr * _sigmoid(gr))).astype(o_ref.dtype)

    def chunk(ci, carry):
        r0 = pl.multiple_of(ci * C, C)
        for bb in range(gq_ref.shape[0]):
            chunk_row(bb, r0)
        return carry

    lax.fori_loop(0, GLA_T // C, chunk, 0)


def _gla(z, gla_w, gla_b, gla_g):
    B, S, _ = z.shape
    gw = jnp.zeros((128, GLA_HEADS * GLA_DK), F32).at[SMALL_GA:SMALL_GA + GLA_RANK].set(gla_w).astype(_MM)
    tril = jnp.asarray(np.tril(np.ones((GLA_CHUNK, GLA_CHUNK), np.float32)), dtype=_MM)
    lm = jnp.asarray(_gla_masks())
    hm = jnp.asarray(np.repeat(np.eye(GLA_HEADS, dtype=np.float32), GLA_DK, axis=1))
    const = lambda shape: pl.BlockSpec(shape, lambda b, t: (0,) * len(shape))
    nb = 4 if B % 4 == 0 else (2 if B % 2 == 0 else 1)
    zspec = lambda w, off: pl.BlockSpec((nb, GLA_T, w), lambda b, t: (b, t, off // w))
    return pl.pallas_call(
        _gla_kernel,
        grid=(B // nb, S // GLA_T),
        in_specs=[zspec(256, Z_GQ), zspec(256, Z_GK), zspec(512, Z_GV), zspec(128, Z_SMALL), zspec(512, Z_GR),
                  const((128, 256)), const((1, 256)), const((1, GLA_DV)), const((GLA_CHUNK, GLA_CHUNK)),
                  const((7, GLA_HEADS * GLA_CHUNK, GLA_CHUNK)), const((GLA_HEADS, 256))],
        out_specs=pl.BlockSpec((nb, GLA_T, 512), lambda b, t: (b, t, 0)),
        out_shape=jax.ShapeDtypeStruct((B, S, 512), _ACT),
        scratch_shapes=[pltpu.VMEM((nb, GLA_DV, GLA_HEADS * GLA_DK), F32)],
        compiler_params=_cp("parallel", "arbitrary"),
        name="gla",
    )(z, z, z, z, z, gw, gla_b.reshape(1, -1).astype(F32), gla_g.reshape(1, -1).astype(F32), tril, lm, hm)


MERGE_T = 512


def _merge_kernel(oa_ref, ob_ref, oc_ref, mg_ref, wb_ref, wo_ref, x_ref, g_ref, b_ref, o_ref):
    merged = None
    for j, r in enumerate((oa_ref, ob_ref, oc_ref)):
        gate = _sigmoid(mg_ref[:, D_MODEL * j:D_MODEL * (j + 1)].astype(F32))
        term = gate * _dot(r[...], wb_ref[j])
        merged = term if merged is None else merged + term
    mix = _dot(merged, wo_ref[...])
    o_ref[...] = _layer_norm(DEEPNORM_ALPHA * x_ref[...] + mix, g_ref[...], b_ref[...])


def _merge(oa, ob, oc, z2, w_branch, w_out, x2, g, b):
    n = x2.shape[0]
    tok = lambda w, blk=0: pl.BlockSpec((MERGE_T, w), lambda i: (i, blk))
    const = lambda shape: pl.BlockSpec(shape, lambda i: (0,) * len(shape))
    return pl.pallas_call(
        _merge_kernel,
        grid=(n // MERGE_T,),
        in_specs=[tok(512), tok(512), tok(512), tok(3072, Z_MERGE // 3072), const((3, 512, D_MODEL)),
                  const((D_MODEL, D_MODEL)), tok(D_MODEL), const((1, D_MODEL)), const((1, D_MODEL))],
        out_specs=tok(D_MODEL),
        out_shape=jax.ShapeDtypeStruct((n, D_MODEL), F32),
        compiler_params=_cp("parallel"),
        name="merge_outproj_ln",
    )(oa, ob, oc, z2, w_branch.astype(_MM), w_out.astype(_MM), x2, g.reshape(1, -1), b.reshape(1, -1))


XA_T = 512


ROW_TILE = 8


def _store_row_tiles(ref, y):
    n = y.shape[0]
    for s in range(ROW_TILE):
        ref[pl.ds(s, n, stride=ROW_TILE), :] = y[:, 128 * s:128 * (s + 1)]


def _load_row_tiles(ref):
    n = ref.shape[0] // ROW_TILE
    return jnp.concatenate([ref[pl.ds(s, n, stride=ROW_TILE), :] for s in range(ROW_TILE)], axis=1)


def _xattn_kernel(x_ref, kv_ref, wq_ref, wo_ref, g_ref, b_ref, rwt_ref, rb_ref, upper_ref,
                  o_ref, o3_ref, ti_ref, gate_ref, rank_ref, cnt_ref, carry):
    x = x_ref[...]
    q = (_dot(x, wq_ref[...]) * (XA_DH ** -0.5)).astype(_MM)
    heads = []
    for h in range(XA_HEADS):
        hs = slice(XA_DH * h, XA_DH * (h + 1))
        s = _dot_nt(q[:, hs], kv_ref[:, hs])
        e = jnp.exp(s - jnp.max(s, axis=-1, keepdims=True))
        p = e / jnp.sum(e, axis=-1, keepdims=True)
        heads.append(_dot(p, kv_ref[:, D_MODEL + XA_DH * h:D_MODEL + XA_DH * (h + 1)]))
    xa = _dot(jnp.concatenate(heads, axis=1), wo_ref[...])
    y = _layer_norm(DEEPNORM_ALPHA * x + xa, g_ref[...], b_ref[...])
    o_ref[...] = y
    _store_row_tiles(o3_ref, y)
    first = jnp.logical_and(pl.program_id(0) == 0, pl.program_id(1) == 0)
    _route_tile(y, first, rwt_ref, rb_ref, upper_ref, ti_ref, gate_ref, rank_ref, cnt_ref, carry)


def _cross_attention_route(x, kv, wq, wo, g, b, router_w, router_b):
    B, S, _ = x.shape
    M = kv.shape[1]
    nt = S // XA_T
    n = B * S
    upper = jnp.asarray(np.triu(np.ones((XA_T, XA_T), np.float32), 1), dtype=_MM)
    bcol = jnp.broadcast_to(router_b.astype(F32)[:, None], (N_EXPERTS, XA_T))
    const = lambda shape: pl.BlockSpec(shape, lambda bb, t: (0,) * len(shape))
    tokspec = pl.BlockSpec((8, XA_T), lambda bb, t: (0, bb * nt + t))
    return pl.pallas_call(
        _xattn_kernel,
        grid=(B, nt),
        in_specs=[pl.BlockSpec((None, XA_T, D_MODEL), lambda bb, t: (bb, t, 0)),
                  pl.BlockSpec((None, M, 2 * D_MODEL), lambda bb, t: (bb, 0, 0)),
                  const((D_MODEL, D_MODEL)), const((D_MODEL, D_MODEL)), const((1, D_MODEL)), const((1, D_MODEL)),
                  const((N_EXPERTS, D_MODEL)), const((N_EXPERTS, XA_T)), const((XA_T, XA_T))],
        out_specs=[pl.BlockSpec((None, XA_T, D_MODEL), lambda bb, t: (bb, t, 0)),
                   pl.BlockSpec((XA_T * ROW_TILE, 128), lambda bb, t: (bb * nt + t, 0)),
                   tokspec, tokspec, tokspec, const((N_EXPERTS, 128))],
        out_shape=[jax.ShapeDtypeStruct((B, S, D_MODEL), F32), jax.ShapeDtypeStruct((n * ROW_TILE, 128), F32),
                   jax.ShapeDtypeStruct((8, n), jnp.int32), jax.ShapeDtypeStruct((8, n), F32),
                   jax.ShapeDtypeStruct((8, n), jnp.int32), jax.ShapeDtypeStruct((N_EXPERTS, 128), F32)],
        scratch_shapes=[pltpu.VMEM((N_EXPERTS, 128), F32)],
        compiler_params=_cp("arbitrary", "arbitrary"),
        name="cross_attention_ln_route",
    )(x, kv, wq.astype(_MM), wo.astype(_MM), g.reshape(1, -1), b.reshape(1, -1), router_w.T.astype(F32), bcol, upper)


def _route_tile(x, first, wt_ref, b_ref, upper_ref, ti_ref, gate_ref, rank_ref, cnt_ref, carry):
    @pl.when(first)
    def _():
        carry[...] = jnp.zeros(carry.shape, F32)

    xh = x.astype(_MM)
    xl = (x - xh.astype(F32)).astype(_MM)
    w = wt_ref[...]
    wh = w.astype(_MM)
    wl = (w - wh.astype(F32)).astype(_MM)
    logits = _dot_nt(wh, xh) + _dot_nt(wh, xl) + _dot_nt(wl, xh) + b_ref[...]
    eidx = lax.broadcasted_iota(jnp.int32, logits.shape, 0)
    v = logits
    tops, idxs, hots = [], [], []
    for _ in range(TOP_K):
        m = jnp.max(v, axis=0, keepdims=True)
        idx = jnp.min(jnp.where(v == m, eidx, N_EXPERTS), axis=0, keepdims=True)
        hot = eidx == idx
        v = jnp.where(hot, -jnp.inf, v)
        tops.append(m)
        idxs.append(idx)
        hots.append(jnp.where(hot, 1.0, 0.0))
    es = [jnp.exp(t - tops[0]) for t in tops]
    den = es[0] + es[1] + es[2] + es[3]
    multihot = hots[0] + hots[1] + hots[2] + hots[3]
    before = jnp.dot(multihot.astype(_MM), upper_ref[...], preferred_element_type=F32) + carry[...][:, 0:1]
    ranks = [jnp.sum(h * before, axis=0, keepdims=True) for h in hots]
    pad = jnp.zeros((8 - TOP_K, x.shape[0]), F32)
    ti_ref[...] = jnp.concatenate(idxs + [pad.astype(jnp.int32)], axis=0)
    gate_ref[...] = jnp.concatenate([e / den for e in es] + [pad], axis=0)
    rank_ref[...] = jnp.concatenate(ranks + [pad], axis=0).astype(jnp.int32)
    carry[...] = carry[...] + jnp.sum(multihot, axis=1, keepdims=True)
    cnt_ref[...] = carry[...]


MOE_SLOTS = 3


def _moe_kernel(blk_e_ref, nvalid_ref, tok0_ref, tok1_ref, tok2_ref, dstp_ref, x_hbm, wgu_ref, bgu_ref, wdn_ref,
                bdn_ref, y_hbm, xbuf, ybuf, wgu_mm, wdn_mm, sem_in, sem_out):
    b = pl.program_id(0)
    nvalid = nvalid_ref[0]
    slot = b % MOE_SLOTS
    nxt = (b + 2) % MOE_SLOTS

    def gather_row(tok_ref, s, r, priority=0):
        src = x_hbm.at[pl.ds(pl.multiple_of(tok_ref[0, r], ROW_TILE), ROW_TILE), :]
        pltpu.make_async_copy(src, xbuf.at[s, pl.ds(ROW_TILE * r, ROW_TILE), :], sem_in.at[s]).start(priority)

    def scatter_row(s, r, priority=0):
        dst = y_hbm.at[pl.ds(pl.multiple_of(dstp_ref[0, r], ROW_TILE), ROW_TILE), :]
        pltpu.make_async_copy(ybuf.at[s, pl.ds(ROW_TILE * r, ROW_TILE), :], dst, sem_out.at[s]).start(priority)

    def wait_in(s):
        pltpu.make_async_copy(x_hbm.at[pl.ds(0, MOE_BLK * ROW_TILE), :], xbuf.at[s], sem_in.at[s]).wait()

    def wait_out(s):
        pltpu.make_async_copy(ybuf.at[s], y_hbm.at[pl.ds(0, MOE_BLK * ROW_TILE), :], sem_out.at[s]).wait()

    @pl.when(b == 0)
    def _():
        ybuf[...] = jnp.zeros(ybuf.shape, F32)

        def body(r, c):
            gather_row(tok0_ref, 0, r)
            gather_row(tok1_ref, 1, r)
            return c
        lax.fori_loop(0, MOE_BLK, body, 0)

    @pl.when(jnp.logical_and(b >= 2, b < nvalid))
    def _():
        wait_out(slot)

    @pl.when(jnp.logical_and(b < nvalid, jnp.logical_or(b == 0, blk_e_ref[b] != blk_e_ref[jnp.maximum(b - 1, 0)])))
    def _():
        wgu_mm[...] = wgu_ref[...].astype(_MM)
        wdn_mm[...] = wdn_ref[...].astype(_MM)

    @pl.when(b < nvalid)
    def _():
        wait_in(slot)
        xb = _load_row_tiles(xbuf.at[slot]).astype(_MM)
        for r in range(MOE_BLK):
            gather_row(tok2_ref, nxt, r, r % 2)
        gu = jnp.dot(xb, wgu_mm[...], preferred_element_type=F32) + bgu_ref[...]
        for r in range(MOE_BLK):
            scatter_row(nxt, r, r % 2)
        g = jnp.minimum(gu[:, :D_MODEL], SWIGLU_LIMIT)
        u = jnp.clip(gu[:, D_MODEL:], -SWIGLU_LIMIT, SWIGLU_LIMIT)
        h = (u + 1.0) * g * _sigmoid(SWIGLU_ALPHA * g)
        _store_row_tiles(ybuf.at[slot], jnp.dot(h.astype(_MM), wdn_mm[...], preferred_element_type=F32) + bdn_ref[...])

    @pl.when(b == nvalid)
    def _():
        def body(r, c):
            scatter_row(nxt, r)
            return c
        lax.fori_loop(0, MOE_BLK, body, 0)
        wait_out(nxt)
        wait_out((b + 1) % MOE_SLOTS)

        @pl.when(nvalid >= 2)
        def _():
            wait_out(slot)
        wait_in(slot)
        wait_in((b + 1) % MOE_SLOTS)


def _moe_experts(x3, slot_tok, slot_dst, blk_e, nvalid, n_out_rows, layer, w_gu, b_gu, w_dn, b_dn):
    n_blocks = blk_e.shape[0]
    tok3 = (slot_tok * ROW_TILE).reshape(n_blocks, 1, MOE_BLK)
    dst3 = (slot_dst * ROW_TILE).reshape(n_blocks + 1, 1, MOE_BLK)
    smem = lambda f: pl.BlockSpec((None, 1, MOE_BLK), f, memory_space=pltpu.SMEM)
    wspec = lambda r, c: pl.BlockSpec((None, None, r, c), lambda b, be, nv: (layer, be[b], 0, 0))
    grid_spec = pltpu.PrefetchScalarGridSpec(
        num_scalar_prefetch=2,
        grid=(n_blocks,),
        in_specs=[
            smem(lambda b, be, nv: (b, 0, 0)),
            smem(lambda b, be, nv: (jnp.minimum(b + 1, n_blocks - 1), 0, 0)),
            smem(lambda b, be, nv: (jnp.minimum(b + 2, n_blocks - 1), 0, 0)),
            smem(lambda b, be, nv: (b, 0, 0)),
            pl.BlockSpec(memory_space=pl.ANY),
            wspec(D_MODEL, 2 * D_MODEL), wspec(1, 2 * D_MODEL), wspec(D_MODEL, D_MODEL), wspec(1, D_MODEL),
        ],
        out_specs=pl.BlockSpec(memory_space=pl.ANY),
        scratch_shapes=[pltpu.VMEM((MOE_SLOTS, MOE_BLK * ROW_TILE, 128), F32),
                        pltpu.VMEM((MOE_SLOTS, MOE_BLK * ROW_TILE, 128), F32),
                        pltpu.VMEM((D_MODEL, 2 * D_MODEL), _MM), pltpu.VMEM((D_MODEL, D_MODEL), _MM),
                        pltpu.SemaphoreType.DMA((MOE_SLOTS,)), pltpu.SemaphoreType.DMA((MOE_SLOTS,))],
    )
    return pl.pallas_call(
        _moe_kernel,
        grid_spec=grid_spec,
        out_shape=jax.ShapeDtypeStruct((n_out_rows * ROW_TILE, 128), F32),
        compiler_params=_cp("arbitrary"),
        name="moe_experts",
    )(blk_e, nvalid, tok3, tok3, tok3, dst3, x3, w_gu, b_gu[:, :, None, :], w_dn, b_dn[:, :, None, :])


def _combine_kernel(y0_ref, y1_ref, y2_ref, y3_ref, gate_ref, x_ref, g_ref, b_ref, o_ref):
    gate = gate_ref[...]
    ff = gate[:, 0:1] * _load_row_tiles(y0_ref)
    for k, r in enumerate((y1_ref, y2_ref, y3_ref)):
        ff = ff + gate[:, k + 1:k + 2] * _load_row_tiles(r)
    o_ref[...] = _layer_norm(DEEPNORM_ALPHA * x_ref[...] + ff, g_ref[...], b_ref[...])


def _moe_combine(ys, gate_t, x2, g, b):
    n = x2.shape[0]
    steps = n // COMB_T
    const = lambda shape: pl.BlockSpec(shape, lambda i: (0,) * len(shape))
    yspec = lambda k: pl.BlockSpec((COMB_T * ROW_TILE, 128), lambda i: (k * steps + i, 0))
    return pl.pallas_call(
        _combine_kernel,
        grid=(steps,),
        in_specs=[yspec(0), yspec(1), yspec(2), yspec(3), pl.BlockSpec((COMB_T, 8), lambda i: (i, 0)),
                  pl.BlockSpec((COMB_T, D_MODEL), lambda i: (i, 0)), const((1, D_MODEL)), const((1, D_MODEL))],
        out_specs=pl.BlockSpec((COMB_T, D_MODEL), lambda i: (i, 0)),
        out_shape=jax.ShapeDtypeStruct((n, D_MODEL), F32),
        compiler_params=_cp("parallel"),
        name="moe_combine_ln",
    )(ys, ys, ys, ys, gate_t, x2, g.reshape(1, -1), b.reshape(1, -1))


def _moe_ffn(x2, x3, top_i, gate, rank, cnt, layer, w_gu, b_gu, w_dn, b_dn, g, b):
    n = x2.shape[0]
    a = n * TOP_K
    counts = cnt[:, 0].astype(jnp.int32)
    padded = (counts + MOE_BLK - 1) // MOE_BLK * MOE_BLK
    pends = jnp.cumsum(padded)
    pstarts = pends - padded
    n_blocks = a // MOE_BLK + N_EXPERTS
    n_slots = n_blocks * MOE_BLK
    eids = jnp.arange(N_EXPERTS, dtype=jnp.int32)
    hot = top_i[:TOP_K, None, :] == eids[None, :, None]
    dest = jnp.sum(jnp.where(hot, pstarts[None, :, None], 0), axis=1) + rank[:TOP_K]
    spare = a + jnp.arange(n_slots, dtype=jnp.int32) % MOE_BLK
    slot_asg = spare.at[dest.reshape(-1)].set(jnp.arange(a, dtype=jnp.int32))
    slot_tok = jnp.where(slot_asg < a, slot_asg % n, 0)
    slot_dst = jnp.concatenate([spare[:MOE_BLK], slot_asg])
    blk_start = jnp.arange(n_blocks, dtype=jnp.int32) * MOE_BLK
    blk_e = jnp.minimum(jnp.sum((pends[None, :] <= blk_start[:, None]).astype(jnp.int32), axis=1), N_EXPERTS - 1)
    nvalid = (pends[-1:] // MOE_BLK).astype(jnp.int32)
    ys = _moe_experts(x3, slot_tok, slot_dst, blk_e, nvalid, a + MOE_BLK, layer, w_gu, b_gu, w_dn, b_dn)
    return _moe_combine(ys, gate.T, x2, g, b)


def _layer(x, mem, tables, w_in, cmp_pe, cmp_w1, cmp_b1, cmp_w2, conv_w, conv_b, conv_g, conv_beta, gla_w, gla_b,
           gla_g, w_branch, w_out, xa_wq, xa_wkv, xa_wo, router_w, router_b, layer, w_gu, b_gu, w_dn, b_dn, ng, nb):
    B, S, D = x.shape
    n = B * S
    x2 = x.reshape(n, D)
    z2 = _matmul(x2, _reorder_w_in(w_in), 1024, ZW // 5, _ACT)
    z = z2.reshape(B, S, ZW)
    kvc = z[:, :, Z_KV:Z_KV + 256].reshape(B, S // 16, 16, 2, 128).transpose(3, 0, 1, 2, 4)
    cmp = _compress(kvc.reshape(2, B, S // 16, 2048), cmp_pe, cmp_w1, cmp_b1, cmp_w2)
    kv4 = jnp.pad(z[:, :, Z_KV + 256:Z_KV + 768], ((0, 0), (WINDOW, 0), (0, 0)))
    oa = _nsa_attention(z, cmp[0], cmp[1], kv4, tables)
    ob = _conformer_conv(z, conv_w, conv_b, conv_g, conv_beta)
    oc = _gla(z, gla_w, gla_b, gla_g)
    x2 = _merge(oa.reshape(n, 512), ob.reshape(n, 512), oc.reshape(n, 512), z2, w_branch, w_out, x2, ng[0], nb[0])
    kv = _matmul(mem.reshape(-1, D), xa_wkv.astype(_MM), 1024, 1024, _ACT).reshape(B, -1, 2 * D)
    x2, x3, top_i, gate, rank, cnt = _cross_attention_route(x2.reshape(B, S, D), kv, xa_wq, xa_wo, ng[1], nb[1],
                                                            router_w, router_b)
    x2 = _moe_ffn(x2.reshape(n, D), x3, top_i, gate, rank, cnt, layer, w_gu, b_gu, w_dn, b_dn, ng[2], nb[2])
    return x2.reshape(B, S, D)


def kernel(x, mem, rel_bias, w_in, cmp_pe, cmp_w1, cmp_b1, cmp_w2, conv_w, conv_b, conv_norm_g, conv_norm_b,
           gla_gate_w, gla_gate_b, gla_norm_g, w_branch, w_out, xa_wq, xa_wkv, xa_wo, router_w, router_b,
           expert_w_gu, expert_b_gu, expert_w_down, expert_b_down, norm_g, norm_b):
    tables = _nsa_tables(rel_bias)
    for l in range(DEPTH):
        x = _layer(x, mem, tables, w_in[l], cmp_pe[l], cmp_w1[l], cmp_b1[l], cmp_w2[l], conv_w[l], conv_b[l],
                   conv_norm_g[l], conv_norm_b[l], gla_gate_w[l], gla_gate_b[l], gla_norm_g[l], w_branch[l],
                   w_out[l], xa_wq[l], xa_wkv[l], xa_wo[l], router_w[l], router_b[l], l, expert_w_gu,
                   expert_b_gu, expert_w_down, expert_b_down, norm_g[l], norm_b[l])
    return x
```

```python
import functools
import math

import numpy as np
import jax
import jax.numpy as jnp
from jax import lax
from jax.experimental import pallas as pl
from jax.experimental.pallas import tpu as pltpu

F32 = jnp.float32
_MM = jnp.bfloat16
_ACT = jnp.bfloat16
_SM = jnp.float32
NEG = -1e30
LOG2E = 1.4426950408889634

D_MODEL = 1024
DEPTH = 2
MEM_LEN = 256
NSA_HEADS = 8
NSA_DH = 64
CMP_LEN = 32
CMP_STRIDE = 16
SEL_BLOCK = 64
SEL_TOP = 16
WINDOW = 512
CONV_WIDTH = 512
CONV_K = 31
GLA_HEADS = 4
GLA_DK = 64
GLA_DV = 128
GLA_RANK = 16
GLA_TAU = 16.0
GLA_CHUNK = 64
REL_BUCKETS = 32
REL_MAX_DIST = 128
XA_HEADS = 4
XA_DH = 256
N_EXPERTS = 32
TOP_K = 4
SWIGLU_ALPHA = 1.702
SWIGLU_LIMIT = 7.0
DEEPNORM_ALPHA = (2 * DEPTH) ** 0.25

Z_MERGE = 0
Z_CONV = 3072
Z_Q = 4096
Z_GV = 4608
Z_GR = 5120
Z_KV = 5632
Z_GQ = 6400
Z_GK = 6656
Z_SMALL = 6912
ZW = 7040
SMALL_GA = 24

_IN_SIZES = (512, 768, 24, 1024, 256, 256, 512, 16, 512, 3072)
_IN_OFF = np.concatenate([[0], np.cumsum(_IN_SIZES)]).astype(int)

VMEM_LIMIT = 56 * 1024 * 1024

NSA_QB = 128
NSA_SUBS = 2
MOE_BLK = 512
COMB_T = 256


def _cp(*sem):
    return pltpu.CompilerParams(dimension_semantics=sem, vmem_limit_bytes=VMEM_LIMIT)


def _dot(a, b):
    return jnp.dot(a.astype(_MM), b.astype(_MM), preferred_element_type=F32)


def _dot_nt(a, b):
    return lax.dot_general(a.astype(_MM), b.astype(_MM), (((1,), (1,)), ((), ())), preferred_element_type=F32)


def _dot_tn(a, b):
    return lax.dot_general(a.astype(_MM), b.astype(_MM), (((0,), (0,)), ((), ())), preferred_element_type=F32)


def _split3(x):
    x1 = x.astype(_MM)
    r1 = x - x1.astype(F32)
    x2 = r1.astype(_MM)
    x3 = (r1 - x2.astype(F32)).astype(_MM)
    return x1, x2, x3


def _dot01_left(m01, x):
    x1, x2, x3 = _split3(x)
    return (jnp.dot(m01, x1, preferred_element_type=F32) + jnp.dot(m01, x2, preferred_element_type=F32)
            + jnp.dot(m01, x3, preferred_element_type=F32))


def _layer_norm(y, g, b, eps=1e-5):
    mu = jnp.mean(y, axis=-1, keepdims=True)
    d = y - mu
    var = jnp.mean(d * d, axis=-1, keepdims=True)
    return d * lax.rsqrt(var + eps) * g + b


def _sigmoid(x):
    return 1.0 / (1.0 + jnp.exp(-x))


def _mm_kernel(x_ref, w_ref, o_ref):
    o_ref[...] = _dot(x_ref[...], w_ref[...]).astype(o_ref.dtype)


def _matmul(x, w, tm, tn, out_dtype):
    m, k = x.shape
    n = w.shape[1]
    tm = min(tm, m)
    return pl.pallas_call(
        _mm_kernel,
        grid=(m // tm, n // tn),
        in_specs=[pl.BlockSpec((tm, k), lambda i, j: (i, 0)), pl.BlockSpec((k, tn), lambda i, j: (0, j))],
        out_specs=pl.BlockSpec((tm, tn), lambda i, j: (i, j)),
        out_shape=jax.ShapeDtypeStruct((m, n), out_dtype),
        compiler_params=_cp("parallel", "parallel"),
        name="matmul",
    )(x, w)


def _reorder_w_in(w_in):
    def cols(i):
        return w_in[:, _IN_OFF[i]:_IN_OFF[i + 1]]
    small = jnp.concatenate([cols(2), cols(7), jnp.zeros((D_MODEL, 128 - 40), w_in.dtype)], axis=1)
    w = jnp.concatenate([cols(9), cols(3), cols(0), cols(6), cols(8), cols(1), cols(4), cols(5), small], axis=1)
    return w.astype(_MM)


def _cmp_kernel(x_ref, pea_ref, peb_ref, wa_ref, wb_ref, b1_ref, w2_ref, o_ref):
    x = x_ref[...].astype(F32)
    a = _dot(x + pea_ref[...], wa_ref[...])
    bm = _dot(x + peb_ref[...], wb_ref[...])
    n = bm.shape[0]
    bs = pltpu.roll(bm, n - 1, axis=0)
    h = jax.nn.gelu(a + bs + b1_ref[...], approximate=True)
    o_ref[...] = _dot(h, w2_ref[...]).astype(o_ref.dtype)


def _compress(kv2, cmp_pe, cmp_w1, cmp_b1, cmp_w2):
    _, B, R, _ = kv2.shape
    eye2 = jnp.eye(2, dtype=F32)

    def half(w1h):
        w = jnp.einsum('rlde,gh->rlgdhe', w1h, eye2)
        return w.reshape(2, 16 * 128, 128).astype(_MM)

    wa = half(cmp_w1[:, :16])
    wb = half(cmp_w1[:, 16:])

    def pe_half(p):
        return jnp.broadcast_to(p[:, :, None, :], (2, 16, 2, 64)).reshape(2, 1, 2048).astype(F32)

    pea = pe_half(cmp_pe[:, :16])
    peb = pe_half(cmp_pe[:, 16:])
    b1 = jnp.tile(cmp_b1, (1, 2)).reshape(2, 1, 128).astype(F32)
    w2 = jnp.einsum('rde,gh->rgdhe', cmp_w2, eye2).reshape(2, 128, 128).astype(_MM)
    return pl.pallas_call(
        _cmp_kernel,
        grid=(2, B),
        in_specs=[
            pl.BlockSpec((None, None, R, 2048), lambda r, b: (r, b, 0, 0)),
            pl.BlockSpec((None, 1, 2048), lambda r, b: (r, 0, 0)),
            pl.BlockSpec((None, 1, 2048), lambda r, b: (r, 0, 0)),
            pl.BlockSpec((None, 2048, 128), lambda r, b: (r, 0, 0)),
            pl.BlockSpec((None, 2048, 128), lambda r, b: (r, 0, 0)),
            pl.BlockSpec((None, 1, 128), lambda r, b: (r, 0, 0)),
            pl.BlockSpec((None, 128, 128), lambda r, b: (r, 0, 0)),
        ],
        out_specs=pl.BlockSpec((None, None, R, 128), lambda r, b: (r, b, 0, 0)),
        out_shape=jax.ShapeDtypeStruct((2, B, R, 128), _ACT),
        compiler_params=_cp("parallel", "parallel"),
        name="nsa_compress",
    )(kv2, pea, peb, wa, wb, b1, w2)


def _t5_bucket_np(dist):
    n = np.maximum(dist, 0)
    exact = REL_BUCKETS // 2
    lr = np.log(np.maximum(n, 1).astype(np.float32) / np.float32(exact)) / np.float32(math.log(REL_MAX_DIST / exact))
    large = exact + (lr * np.float32(REL_BUCKETS - exact)).astype(np.int32)
    return np.where(n < exact, n, np.minimum(large, REL_BUCKETS - 1))


def _nsa_tables(rel_bias):
    rel = (rel_bias - rel_bias[REL_BUCKETS - 1:REL_BUCKETS]).astype(F32) * LOG2E
    q = np.arange(NSA_QB)

    def table(dist, valid, fill):
        hot = jnp.asarray(_t5_bucket_np(dist), jnp.int32)[:, :, None] == jnp.arange(REL_BUCKETS)[None, None, :]
        t = jnp.sum(jnp.where(hot[..., None], rel[None, None], 0.0), axis=2)
        t = jnp.where(valid[:, :, None], t, fill)
        return jnp.transpose(t, (0, 2, 1)).reshape(dist.shape[0], NSA_HEADS * NSA_QB)

    cc = np.arange(24) - 16
    d_c = q[None, :] - CMP_STRIDE * cc[:, None] - (CMP_LEN - 1)
    t_cmp = table(d_c, d_c >= 0, 0.0)
    ko = np.arange(256)
    d_n = q[None, :] + 128 - ko[:, None]
    t_near = table(d_n, d_n >= 0, NEG)
    kw = np.arange(WINDOW + NSA_QB)
    d_w = q[None, :] + WINDOW - kw[:, None]
    in_win = (d_w >= 0) & (d_w < WINDOW)
    t_win = jnp.stack([table(d_w, in_win & (kw[:, None] >= WINDOW - NSA_QB * v), NEG) for v in range(5)])
    return t_cmp, t_near, t_win


def _nsa_kernel(q_ref, gate_ref, kcmp_ref, vcmp_ref, ksel_ref, vsel_ref, kwin_ref, vwin_ref,
                cmpa_ref, tcmp_ref, tnear_ref, *rest):
    twin_refs = rest[:NSA_SUBS]
    ovl_ref, out_ref, s_scr, v_scr, sel_scr, far_scr = rest[NSA_SUBS:]
    step = pl.program_id(1)
    ncmp = kcmp_ref.shape[0]
    nblk = ovl_ref.shape[0]
    zeros64 = jnp.zeros((64, NSA_QB), F32)
    jidx = lax.broadcasted_iota(jnp.int32, (nblk, NSA_QB), 0)
    qidx = lax.broadcasted_iota(jnp.int32, (nblk, NSA_QB), 1)
    sub8 = lax.broadcasted_iota(jnp.int32, (8, NSA_QB), 0)
    neg8 = jnp.full((8, NSA_QB), NEG, F32)

    def mask_rows(scr, first_block, n):
        rows = [jnp.broadcast_to(scr[pl.ds(8 + first_block + b, 1), :], (SEL_BLOCK, NSA_QB)) for b in range(n)]
        return jnp.concatenate(rows, axis=0)

    def add_mask(s, m):
        return jnp.concatenate([s[:, 128 * h:128 * h + 128] + m for h in range(4)], axis=1)

    def col_max(x):
        if x.dtype == F32:
            return jnp.max(x, axis=0, keepdims=True)
        y = jnp.max(x.reshape(x.shape[0] // 16, 16, x.shape[1]), axis=0)
        return jnp.max(y.astype(F32), axis=0, keepdims=True)

    def col_sum(e):
        if e.dtype == F32:
            return jnp.sum(e, axis=0, keepdims=True)
        return jnp.dot(jnp.ones((8, e.shape[0]), e.dtype), e, preferred_element_type=F32)[0:1]

    def probs(s, table, mask, m_old=None):
        x = s.astype(_SM)
        if table is not None:
            x = x + table
        if mask is not None:
            x = add_mask(x, mask.astype(_SM))
        m = col_max(x)
        if m_old is not None:
            m = jnp.maximum(m_old, m)
        e = jnp.exp2(x - m.astype(_SM))
        return m, e, col_sum(e)

    q_gs, o_cs, o_ws, gts, state = [], [], [], [], []
    for sub, g in [(sub, g) for sub in range(NSA_SUBS) for g in range(2)]:
        i = NSA_SUBS * step + sub
        qrows = slice(NSA_QB * sub, NSA_QB * (sub + 1))
        if g == 0:
            qt = (q_ref[qrows, :].astype(F32) * (NSA_DH ** -0.5 * LOG2E)).T
            gts.append(_sigmoid(gate_ref[qrows, :].astype(F32)).T)
            cur = 2 * i + (qidx >= SEL_BLOCK).astype(jnp.int32)
            forced = (jidx == 0) | (jidx == cur) | (jidx == cur - 1)
            future = jidx > cur
            s_scr[sub, 0:16, :] = jnp.zeros((16, 4 * NSA_QB), F32)
        twin_ref = twin_refs[sub]
        blocks = []
        for h in range(4):
            r = qt[64 * (4 * g + h):64 * (4 * g + h) + 64]
            blocks.append(jnp.concatenate([r, zeros64] if g == 0 else [zeros64, r], axis=0))
        q_g = jnp.concatenate(blocks, axis=1).astype(_MM)
        q_gs.append(q_g)
        cs = slice(512 * g, 512 * g + 512)

        s_scr[sub, 16:16 + ncmp, :] = _dot(kcmp_ref[...], q_g)
        w0 = pl.multiple_of(8 * i, 8)
        s_scr[sub, pl.ds(w0, 24), :] = s_scr[sub, pl.ds(w0, 24), :] + tcmp_ref[:, cs]
        sc = jnp.where(cmpa_ref[...] <= NSA_QB * i, s_scr[sub, 16:16 + ncmp, :], NEG)
        m = jnp.maximum(jnp.max(sc, axis=0, keepdims=True), 0.1 * NEG)
        e = jnp.exp2(sc - m)
        l = jnp.sum(e, axis=0, keepdims=True)
        pc = e / jnp.maximum(l, 1e-30)
        o_cs.append(_dot_tn(vcmp_ref[...], pc))

        psum = pc[:, 0:128] + pc[:, 128:256] + pc[:, 256:384] + pc[:, 384:512]
        imp = _dot01_left(ovl_ref[...], psum)
        v = jnp.where(forced, 1e30, jnp.where(future, -1.0, imp))
        v_scr[sub, g] = v
        sel_scr[sub, g, 0:8, :] = neg8
        far_scr[sub, g, 0:8, :] = neg8
        for r8 in range(0, nblk, 8):
            vr = v[r8:r8 + 8]
            cnt = jnp.zeros((8, NSA_QB), F32)
            for jp in range(nblk):
                row = jnp.broadcast_to(v_scr[sub, g, pl.ds(jp, 1), :], (8, NSA_QB))
                if jp < r8:
                    cnt = cnt + jnp.where(row >= vr, 1.0, 0.0)
                elif jp >= r8 + 8:
                    cnt = cnt + jnp.where(row > vr, 1.0, 0.0)
                else:
                    cnt = cnt + jnp.where(sub8 > jp - r8, jnp.where(row >= vr, 1.0, 0.0),
                                          jnp.where(row > vr, 1.0, 0.0))
            sel = jnp.where(cnt < float(min(SEL_TOP, nblk)), 0.0, NEG)
            sel_scr[sub, g, 8 + r8:16 + r8, :] = sel
            far_scr[sub, g, 8 + r8:16 + r8, :] = jnp.where(jidx[r8:r8 + 8] <= 2 * i - 3, sel, NEG)

        n0 = pl.multiple_of(NSA_QB * i + WINDOW - 128, 128)
        m_s, e, l_s = probs(_dot(ksel_ref[pl.ds(n0, 256), :], q_g), tnear_ref[:, cs],
                            mask_rows(sel_scr.at[sub, g], 2 * i - 2, 4))
        state += [m_s, l_s, _dot_tn(vsel_ref[pl.ds(n0, 256), :], e)]

        w_start = pl.multiple_of(NSA_QB * i, 128)
        _, e, l_w = probs(_dot(kwin_ref[pl.ds(w_start, WINDOW + NSA_QB), :], q_g), twin_ref[:, cs], None)
        o_ws.append(_dot_tn(vwin_ref[pl.ds(w_start, WINDOW + NSA_QB), :], e) / l_w)

    def far_chunk(c, carry):
        k0 = pl.multiple_of(WINDOW + 512 * c, 512)
        kf = ksel_ref[pl.ds(k0, 512), :]
        vf = vsel_ref[pl.ds(k0, 512), :]
        new = []
        for n in range(2 * NSA_SUBS):
            m_o, l_o, a_o = carry[3 * n:3 * n + 3]
            m_n, ef, l_c = probs(_dot(kf, q_gs[n]), None, mask_rows(far_scr.at[n // 2, n % 2], 8 * c, 8), m_o)
            alpha = jnp.exp2(m_o - m_n)
            new += [m_n, alpha * l_o + l_c, alpha * a_o + _dot_tn(vf, ef)]
        return tuple(new)

    i_last = NSA_SUBS * step + NSA_SUBS - 1
    state = lax.fori_loop(0, (2 * i_last + 5) // 8, far_chunk, tuple(state))

    for sub in range(NSA_SUBS):
        gt = gts[sub]
        out_rows = []
        for g in range(2):
            n = 2 * sub + g
            o_s = state[3 * n + 2] / state[3 * n + 1]
            for h in range(4):
                hs = slice(128 * h, 128 * h + 128)
                ds_ = slice(64 * g, 64 * g + 64)
                gi = (4 * g + h) * 3
                out_rows.append(gt[gi:gi + 1, :] * o_cs[n][ds_, hs] + gt[gi + 1:gi + 2, :] * o_s[ds_, hs]
                                + gt[gi + 2:gi + 3, :] * o_ws[n][ds_, hs])
        out_ref[NSA_QB * sub:NSA_QB * (sub + 1), :] = jnp.concatenate(out_rows, axis=0).T.astype(out_ref.dtype)


def _nsa_attention(z, kcmp, vcmp, kv4, tables):
    B, S, _ = z.shape
    qs = NSA_QB * NSA_SUBS
    n_steps = S // qs
    nblk = S // SEL_BLOCK
    ncmp = kcmp.shape[1]
    t_cmp, t_near, t_win = tables
    cmpa = (CMP_STRIDE * np.arange(ncmp)[:, None] + (CMP_LEN - 1) - np.arange(NSA_QB)[None, :]).astype(np.int32)
    cmpa = jnp.asarray(np.tile(cmpa, (1, 4)))
    c = np.arange(ncmp)[None, :]
    j = np.arange(nblk)[:, None]
    ovl = ((CMP_STRIDE * c < (j + 1) * SEL_BLOCK) & (CMP_STRIDE * c + CMP_LEN > j * SEL_BLOCK)
           & (c < (S - CMP_LEN) // CMP_STRIDE + 1))
    ovl = jnp.asarray(ovl.astype(np.float32), dtype=_MM)
    sp = S + WINDOW
    const = lambda shape: pl.BlockSpec(shape, lambda b, i: (0,) * len(shape))
    return pl.pallas_call(
        _nsa_kernel,
        grid=(B, n_steps),
        in_specs=[
            pl.BlockSpec((None, qs, 512), lambda b, i: (b, i, Z_Q // 512)),
            pl.BlockSpec((None, qs, 128), lambda b, i: (b, i, Z_SMALL // 128)),
            pl.BlockSpec((None, ncmp, 128), lambda b, i: (b, 0, 0)),
            pl.BlockSpec((None, ncmp, 128), lambda b, i: (b, 0, 0)),
            pl.BlockSpec((None, sp, 128), lambda b, i: (b, 0, 0)),
            pl.BlockSpec((None, sp, 128), lambda b, i: (b, 0, 1)),
            pl.BlockSpec((None, sp, 128), lambda b, i: (b, 0, 2)),
            pl.BlockSpec((None, sp, 128), lambda b, i: (b, 0, 3)),
            const((ncmp, 512)), const((24, 1024)), const((256, 1024)),
        ] + [pl.BlockSpec((None, WINDOW + NSA_QB, 1024), functools.partial(
            lambda b, i, sub: (jnp.minimum(NSA_SUBS * i + sub, 4), 0, 0), sub=sub)) for sub in range(NSA_SUBS)] + [
            const((nblk, ncmp)),
        ],
        out_specs=pl.BlockSpec((None, qs, 512), lambda b, i: (b, i, 0)),
        out_shape=jax.ShapeDtypeStruct((B, S, 512), _ACT),
        scratch_shapes=[pltpu.VMEM((NSA_SUBS, 16 + ncmp, 512), F32), pltpu.VMEM((NSA_SUBS, 2, nblk, NSA_QB), F32),
                        pltpu.VMEM((NSA_SUBS, 2, 8 + nblk, NSA_QB), F32),
                        pltpu.VMEM((NSA_SUBS, 2, 8 + nblk, NSA_QB), F32)],
        compiler_params=_cp("parallel", "arbitrary"),
        name="nsa_attention",
    )(z, z, kcmp, vcmp, kv4, kv4, kv4, kv4, cmpa, t_cmp, t_near.astype(_SM),
      *([t_win.astype(_SM)] * NSA_SUBS), ovl)


CONV_T = 256
CONV_SUB = 64
CONV_HALO = 32


def _conv_kernel(z_ref, w_ref, b_ref, g_ref, beta_ref, o_ref, ubuf):
    t = pl.program_id(1)

    @pl.when(t == 0)
    def _():
        ubuf[0:CONV_HALO, :] = jnp.zeros((CONV_HALO, CONV_WIDTH), F32)
        ubuf[CONV_HALO + CONV_T:, :] = jnp.zeros((8, CONV_WIDTH), F32)

    zt = z_ref[...].astype(F32)
    ubuf[CONV_HALO:CONV_HALO + CONV_T, :] = zt[:, :CONV_WIDTH] * _sigmoid(zt[:, CONV_WIDTH:])
    for r in range(0, CONV_T, CONV_SUB):
        acc = None
        for s in range(8):
            p = None
            for a in range((CONV_K + 1) // 8 + 1):
                k = 8 * a + s - (CONV_HALO - CONV_K + 1)
                if 0 <= k < CONV_K:
                    term = w_ref[k:k + 1, :] * ubuf[r + 8 * a:r + 8 * a + CONV_SUB + 8, :]
                    p = term if p is None else p + term
            acc = p[s:s + CONV_SUB] if acc is None else acc + p[s:s + CONV_SUB]
        y = _layer_norm(acc + b_ref[...], g_ref[...], beta_ref[...])
        o_ref[r:r + CONV_SUB, :] = (y * _sigmoid(y)).astype(o_ref.dtype)
    ubuf[0:CONV_HALO, :] = ubuf[CONV_T:CONV_T + CONV_HALO, :]


def _conformer_conv(z, conv_w, conv_b, conv_g, conv_beta):
    B, S, _ = z.shape
    w = jnp.concatenate([conv_w, jnp.zeros((1, CONV_WIDTH), conv_w.dtype)], axis=0).astype(F32)
    row = lambda a: a.reshape(1, CONV_WIDTH).astype(F32)
    const = lambda shape: pl.BlockSpec(shape, lambda b, t: (0, 0))
    return pl.pallas_call(
        _conv_kernel,
        grid=(B, S // CONV_T),
        in_specs=[pl.BlockSpec((None, CONV_T, 1024), lambda b, t: (b, t, Z_CONV // 1024)),
                  const((CONV_K + 1, CONV_WIDTH)), const((1, CONV_WIDTH)), const((1, CONV_WIDTH)),
                  const((1, CONV_WIDTH))],
        out_specs=pl.BlockSpec((None, CONV_T, CONV_WIDTH), lambda b, t: (b, t, 0)),
        out_shape=jax.ShapeDtypeStruct((B, S, CONV_WIDTH), _ACT),
        scratch_shapes=[pltpu.VMEM((CONV_HALO + CONV_T + 8, CONV_WIDTH), F32)],
        compiler_params=_cp("parallel", "arbitrary"),
        name="conformer_conv",
    )(z, w, row(conv_b), row(conv_g), row(conv_beta))


GLA_T = 256
_GLA_LEVELS = (1, 2, 4, 8, 16, 32)


def _gla_masks():
    t = np.arange(GLA_CHUNK)[:, None]
    s = np.arange(GLA_CHUNK)[None, :]
    ms = [(t == s)]
    for c in _GLA_LEVELS:
        ms.append(((t // c) % 2 == 1) & (s // c == t // c - 1))
    m = np.stack(ms).astype(np.float32)
    return np.tile(m, (1, GLA_HEADS, 1))


def _gla_kernel(gq_ref, gk_ref, gv_ref, sm_ref, gr_ref, gw_ref, gb_ref, gg_ref, tril_ref, lm_ref, hm_ref,
                o_ref, state):
    @pl.when(pl.program_id(1) == 0)
    def _():
        state[...] = jnp.zeros(state.shape, F32)

    C = GLA_CHUNK
    tidx = lax.broadcasted_iota(jnp.int32, (C, GLA_HEADS * GLA_DK), 0)

    def heads_dot_nt(x, y):
        out = []
        for p in range(GLA_HEADS // 2):
            xp = x[:, 128 * p:128 * (p + 1)]
            xs = jnp.concatenate([xp * hm_ref[0:1, 0:128], xp * hm_ref[1:2, 0:128]], axis=0)
            out.append(_dot_nt(xs, y[:, 128 * p:128 * (p + 1)]))
        return jnp.concatenate(out, axis=0)

    def chunk_row(bb, r0):
        q = gq_ref[bb, pl.ds(r0, C), :].astype(F32) * (GLA_DK ** -0.5)
        k = gk_ref[bb, pl.ds(r0, C), :].astype(F32)
        v = gv_ref[bb, pl.ds(r0, C), :]
        pre = _dot(sm_ref[bb, pl.ds(r0, C), :], gw_ref[...]) + gb_ref[...]
        la = (jnp.minimum(pre, 0.0) - jnp.log1p(jnp.exp(-jnp.abs(pre)))) * (1.0 / GLA_TAU)
        b = _dot01_left(tril_ref[...], la)
        attn = heads_dot_nt(q, k) * lm_ref[0]
        bstart = b
        bnext = pltpu.roll(b, C - 1, axis=0)
        for li, c in enumerate(_GLA_LEVELS):
            odd = (tidx // c) % 2 == 1
            q_l = jnp.where(odd, q * jnp.exp(b - bstart), 0.0)
            k_l = jnp.where(odd, 0.0, k * jnp.exp(bnext - b))
            attn = attn + heads_dot_nt(q_l, k_l) * lm_ref[li + 1]
            half = (tidx % (2 * c)) < c
            bstart = jnp.where(half, bstart, pltpu.roll(bstart, c, axis=0))
            bnext = jnp.where(half, pltpu.roll(bnext, C - c, axis=0), bnext)
        st = state[bb]
        r_intra = _dot(attn, v)
        r_inter = heads_dot_nt(q * jnp.exp(b), st)
        b_last = b[C - 1:C, :]
        upd = _dot_tn(v, k * jnp.exp(b_last - b))
        new_st = st * jnp.exp(b_last) + jnp.concatenate(
            [upd[128 * h:128 * h + 128, 64 * h:64 * h + 64] for h in range(GLA_HEADS)], axis=1)
        state[bb] = new_st
        outs = []
        for h in range(GLA_HEADS):
            o = r_intra[64 * h:64 * h + 64, 128 * h:128 * h + 128] + r_inter[64 * h:64 * h + 64, :]
            ms = jnp.mean(o * o, axis=-1, keepdims=True)
            outs.append(o * lax.rsqrt(ms + 1e-6) * gg_ref[...])
        gr = gr_ref[bb, pl.ds(r0, C), :].astype(F32)
        o_ref[bb, pl.ds(r0, C), :] = (jnp.concatenate(outs, axis=1) * (gr * _sigmoid(gr))).astype(o_ref.dtype)

    def chunk(ci, carry):
        r0 = pl.multiple_of(ci * C, C)
        for bb in range(gq_ref.shape[0]):
            chunk_row(bb, r0)
        return carry

    lax.fori_loop(0, GLA_T // C, chunk, 0)


def _gla(z, gla_w, gla_b, gla_g):
    B, S, _ = z.shape
    gw = jnp.zeros((128, GLA_HEADS * GLA_DK), F32).at[SMALL_GA:SMALL_GA + GLA_RANK].set(gla_w).astype(_MM)
    tril = jnp.asarray(np.tril(np.ones((GLA_CHUNK, GLA_CHUNK), np.float32)), dtype=_MM)
    lm = jnp.asarray(_gla_masks())
    hm = jnp.asarray(np.repeat(np.eye(GLA_HEADS, dtype=np.float32), GLA_DK, axis=1))
    const = lambda shape: pl.BlockSpec(shape, lambda b, t: (0,) * len(shape))
    nb = 4 if B % 4 == 0 else (2 if B % 2 == 0 else 1)
    zspec = lambda w, off: pl.BlockSpec((nb, GLA_T, w), lambda b, t: (b, t, off // w))
    return pl.pallas_call(
        _gla_kernel,
        grid=(B // nb, S // GLA_T),
        in_specs=[zspec(256, Z_GQ), zspec(256, Z_GK), zspec(512, Z_GV), zspec(128, Z_SMALL), zspec(512, Z_GR),
                  const((128, 256)), const((1, 256)), const((1, GLA_DV)), const((GLA_CHUNK, GLA_CHUNK)),
                  const((7, GLA_HEADS * GLA_CHUNK, GLA_CHUNK)), const((GLA_HEADS, 256))],
        out_specs=pl.BlockSpec((nb, GLA_T, 512), lambda b, t: (b, t, 0)),
        out_shape=jax.ShapeDtypeStruct((B, S, 512), _ACT),
        scratch_shapes=[pltpu.VMEM((nb, GLA_DV, GLA_HEADS * GLA_DK), F32)],
        compiler_params=_cp("parallel", "arbitrary"),
        name="gla",
    )(z, z, z, z, z, gw, gla_b.reshape(1, -1).astype(F32), gla_g.reshape(1, -1).astype(F32), tril, lm, hm)


MERGE_T = 512


def _merge_kernel(oa_ref, ob_ref, oc_ref, mg_ref, wb_ref, wo_ref, x_ref, g_ref, b_ref, o_ref):
    merged = None
    for j, r in enumerate((oa_ref, ob_ref, oc_ref)):
        gate = _sigmoid(mg_ref[:, D_MODEL * j:D_MODEL * (j + 1)].astype(F32))
        term = gate * _dot(r[...], wb_ref[j])
        merged = term if merged is None else merged + term
    mix = _dot(merged, wo_ref[...])
    o_ref[...] = _layer_norm(DEEPNORM_ALPHA * x_ref[...] + mix, g_ref[...], b_ref[...])


def _merge(oa, ob, oc, z2, w_branch, w_out, x2, g, b):
    n = x2.shape[0]
    tok = lambda w, blk=0: pl.BlockSpec((MERGE_T, w), lambda i: (i, blk))
    const = lambda shape: pl.BlockSpec(shape, lambda i: (0,) * len(shape))
    return pl.pallas_call(
        _merge_kernel,
        grid=(n // MERGE_T,),
        in_specs=[tok(512), tok(512), tok(512), tok(3072, Z_MERGE // 3072), const((3, 512, D_MODEL)),
                  const((D_MODEL, D_MODEL)), tok(D_MODEL), const((1, D_MODEL)), const((1, D_MODEL))],
        out_specs=tok(D_MODEL),
        out_shape=jax.ShapeDtypeStruct((n, D_MODEL), F32),
        compiler_params=_cp("parallel"),
        name="merge_outproj_ln",
    )(oa, ob, oc, z2, w_branch.astype(_MM), w_out.astype(_MM), x2, g.reshape(1, -1), b.reshape(1, -1))


XA_T = 512


ROW_TILE = 8


def _store_row_tiles(ref, y):
    n = y.shape[0]
    for s in range(ROW_TILE):
        ref[pl.ds(s, n, stride=ROW_TILE), :] = y[:, 128 * s:128 * (s + 1)]


def _load_row_tiles(ref):
    n = ref.shape[0] // ROW_TILE
    return jnp.concatenate([ref[pl.ds(s, n, stride=ROW_TILE), :] for s in range(ROW_TILE)], axis=1)


def _xattn_kernel(x_ref, kv_ref, wq_ref, wo_ref, g_ref, b_ref, rwt_ref, rb_ref, upper_ref,
                  o_ref, o3_ref, ti_ref, gate_ref, rank_ref, cnt_ref, carry):
    x = x_ref[...]
    q = (_dot(x, wq_ref[...]) * (XA_DH ** -0.5)).astype(_MM)
    heads = []
    for h in range(XA_HEADS):
        hs = slice(XA_DH * h, XA_DH * (h + 1))
        s = _dot_nt(q[:, hs], kv_ref[:, hs])
        e = jnp.exp(s - jnp.max(s, axis=-1, keepdims=True))
        p = e / jnp.sum(e, axis=-1, keepdims=True)
        heads.append(_dot(p, kv_ref[:, D_MODEL + XA_DH * h:D_MODEL + XA_DH * (h + 1)]))
    xa = _dot(jnp.concatenate(heads, axis=1), wo_ref[...])
    y = _layer_norm(DEEPNORM_ALPHA * x + xa, g_ref[...], b_ref[...])
    o_ref[...] = y
    _store_row_tiles(o3_ref, y)
    first = jnp.logical_and(pl.program_id(0) == 0, pl.program_id(1) == 0)
    _route_tile(y, first, rwt_ref, rb_ref, upper_ref, ti_ref, gate_ref, rank_ref, cnt_ref, carry)


def _cross_attention_route(x, kv, wq, wo, g, b, router_w, router_b):
    B, S, _ = x.shape
    M = kv.shape[1]
    nt = S // XA_T
    n = B * S
    upper = jnp.asarray(np.triu(np.ones((XA_T, XA_T), np.float32), 1), dtype=_MM)
    bcol = jnp.broadcast_to(router_b.astype(F32)[:, None], (N_EXPERTS, XA_T))
    const = lambda shape: pl.BlockSpec(shape, lambda bb, t: (0,) * len(shape))
    tokspec = pl.BlockSpec((8, XA_T), lambda bb, t: (0, bb * nt + t))
    return pl.pallas_call(
        _xattn_kernel,
        grid=(B, nt),
        in_specs=[pl.BlockSpec((None, XA_T, D_MODEL), lambda bb, t: (bb, t, 0)),
                  pl.BlockSpec((None, M, 2 * D_MODEL), lambda bb, t: (bb, 0, 0)),
                  const((D_MODEL, D_MODEL)), const((D_MODEL, D_MODEL)), const((1, D_MODEL)), const((1, D_MODEL)),
                  const((N_EXPERTS, D_MODEL)), const((N_EXPERTS, XA_T)), const((XA_T, XA_T))],
        out_specs=[pl.BlockSpec((None, XA_T, D_MODEL), lambda bb, t: (bb, t, 0)),
                   pl.BlockSpec((XA_T * ROW_TILE, 128), lambda bb, t: (bb * nt + t, 0)),
                   tokspec, tokspec, tokspec, const((N_EXPERTS, 128))],
        out_shape=[jax.ShapeDtypeStruct((B, S, D_MODEL), F32), jax.ShapeDtypeStruct((n * ROW_TILE, 128), F32),
                   jax.ShapeDtypeStruct((8, n), jnp.int32), jax.ShapeDtypeStruct((8, n), F32),
                   jax.ShapeDtypeStruct((8, n), jnp.int32), jax.ShapeDtypeStruct((N_EXPERTS, 128), F32)],
        scratch_shapes=[pltpu.VMEM((N_EXPERTS, 128), F32)],
        compiler_params=_cp("arbitrary", "arbitrary"),
        name="cross_attention_ln_route",
    )(x, kv, wq.astype(_MM), wo.astype(_MM), g.reshape(1, -1), b.reshape(1, -1), router_w.T.astype(F32), bcol, upper)


def _route_tile(x, first, wt_ref, b_ref, upper_ref, ti_ref, gate_ref, rank_ref, cnt_ref, carry):
    @pl.when(first)
    def _():
        carry[...] = jnp.zeros(carry.shape, F32)

    xh = x.astype(_MM)
    xl = (x - xh.astype(F32)).astype(_MM)
    w = wt_ref[...]
    wh = w.astype(_MM)
    wl = (w - wh.astype(F32)).astype(_MM)
    logits = _dot_nt(wh, xh) + _dot_nt(wh, xl) + _dot_nt(wl, xh) + b_ref[...]
    eidx = lax.broadcasted_iota(jnp.int32, logits.shape, 0)
    v = logits
    tops, idxs, hots = [], [], []
    for _ in range(TOP_K):
        m = jnp.max(v, axis=0, keepdims=True)
        idx = jnp.min(jnp.where(v == m, eidx, N_EXPERTS), axis=0, keepdims=True)
        hot = eidx == idx
        v = jnp.where(hot, -jnp.inf, v)
        tops.append(m)
        idxs.append(idx)
        hots.append(jnp.where(hot, 1.0, 0.0))
    es = [jnp.exp(t - tops[0]) for t in tops]
    den = es[0] + es[1] + es[2] + es[3]
    multihot = hots[0] + hots[1] + hots[2] + hots[3]
    before = jnp.dot(multihot.astype(_MM), upper_ref[...], preferred_element_type=F32) + carry[...][:, 0:1]
    ranks = [jnp.sum(h * before, axis=0, keepdims=True) for h in hots]
    pad = jnp.zeros((8 - TOP_K, x.shape[0]), F32)
    ti_ref[...] = jnp.concatenate(idxs + [pad.astype(jnp.int32)], axis=0)
    gate_ref[...] = jnp.concatenate([e / den for e in es] + [pad], axis=0)
    rank_ref[...] = jnp.concatenate(ranks + [pad], axis=0).astype(jnp.int32)
    carry[...] = carry[...] + jnp.sum(multihot, axis=1, keepdims=True)
    cnt_ref[...] = carry[...]


MOE_SLOTS = 3


def _moe_kernel(blk_e_ref, nvalid_ref, tok0_ref, tok1_ref, tok2_ref, dstp_ref, x_hbm, wgu_ref, bgu_ref, wdn_ref,
                bdn_ref, y_hbm, xbuf, ybuf, wgu_mm, wdn_mm, sem_in, sem_out):
    b = pl.program_id(0)
    nvalid = nvalid_ref[0]
    slot = b % MOE_SLOTS
    nxt = (b + 2) % MOE_SLOTS

    def gather_row(tok_ref, s, r, priority=0):
        src = x_hbm.at[pl.ds(pl.multiple_of(tok_ref[0, r], ROW_TILE), ROW_TILE), :]
        pltpu.make_async_copy(src, xbuf.at[s, pl.ds(ROW_TILE * r, ROW_TILE), :], sem_in.at[s]).start(priority)

    def scatter_row(s, r, priority=0):
        dst = y_hbm.at[pl.ds(pl.multiple_of(dstp_ref[0, r], ROW_TILE), ROW_TILE), :]
        pltpu.make_async_copy(ybuf.at[s, pl.ds(ROW_TILE * r, ROW_TILE), :], dst, sem_out.at[s]).start(priority)

    def wait_in(s):
        pltpu.make_async_copy(x_hbm.at[pl.ds(0, MOE_BLK * ROW_TILE), :], xbuf.at[s], sem_in.at[s]).wait()

    def wait_out(s):
        pltpu.make_async_copy(ybuf.at[s], y_hbm.at[pl.ds(0, MOE_BLK * ROW_TILE), :], sem_out.at[s]).wait()

    @pl.when(b == 0)
    def _():
        ybuf[...] = jnp.zeros(ybuf.shape, F32)

        def body(r, c):
            gather_row(tok0_ref, 0, r)
            gather_row(tok1_ref, 1, r)
            return c
        lax.fori_loop(0, MOE_BLK, body, 0)

    @pl.when(jnp.logical_and(b >= 2, b < nvalid))
    def _():
        wait_out(slot)

    @pl.when(jnp.logical_and(b < nvalid, jnp.logical_or(b == 0, blk_e_ref[b] != blk_e_ref[jnp.maximum(b - 1, 0)])))
    def _():
        wgu_mm[...] = wgu_ref[...].astype(_MM)
        wdn_mm[...] = wdn_ref[...].astype(_MM)

    @pl.when(b < nvalid)
    def _():
        wait_in(slot)
        xb = _load_row_tiles(xbuf.at[slot]).astype(_MM)
        for r in range(MOE_BLK):
            gather_row(tok2_ref, nxt, r, r % 2)
        gu = jnp.dot(xb, wgu_mm[...], preferred_element_type=F32) + bgu_ref[...]
        for r in range(MOE_BLK):
            scatter_row(nxt, r, r % 2)
        g = jnp.minimum(gu[:, :D_MODEL], SWIGLU_LIMIT)
        u = jnp.clip(gu[:, D_MODEL:], -SWIGLU_LIMIT, SWIGLU_LIMIT)
        h = (u + 1.0) * g * _sigmoid(SWIGLU_ALPHA * g)
        _store_row_tiles(ybuf.at[slot], jnp.dot(h.astype(_MM), wdn_mm[...], preferred_element_type=F32) + bdn_ref[...])

    @pl.when(b == nvalid)
    def _():
        def body(r, c):
            scatter_row(nxt, r)
            return c
        lax.fori_loop(0, MOE_BLK, body, 0)
        wait_out(nxt)
        wait_out((b + 1) % MOE_SLOTS)

        @pl.when(nvalid >= 2)
        def _():
            wait_out(slot)
        wait_in(slot)
        wait_in((b + 1) % MOE_SLOTS)


def _moe_experts(x3, slot_tok, slot_dst, blk_e, nvalid, n_out_rows, layer, w_gu, b_gu, w_dn, b_dn):
    n_blocks = blk_e.shape[0]
    tok3 = (slot_tok * ROW_TILE).reshape(n_blocks, 1, MOE_BLK)
    dst3 = (slot_dst * ROW_TILE).reshape(n_blocks + 1, 1, MOE_BLK)
    smem = lambda f: pl.BlockSpec((None, 1, MOE_BLK), f, memory_space=pltpu.SMEM)
    wspec = lambda r, c: pl.BlockSpec((None, None, r, c), lambda b, be, nv: (layer, be[b], 0, 0))
    grid_spec = pltpu.PrefetchScalarGridSpec(
        num_scalar_prefetch=2,
        grid=(n_blocks,),
        in_specs=[
            smem(lambda b, be, nv: (b, 0, 0)),
            smem(lambda b, be, nv: (jnp.minimum(b + 1, n_blocks - 1), 0, 0)),
            smem(lambda b, be, nv: (jnp.minimum(b + 2, n_blocks - 1), 0, 0)),
            smem(lambda b, be, nv: (b, 0, 0)),
            pl.BlockSpec(memory_space=pl.ANY),
            wspec(D_MODEL, 2 * D_MODEL), wspec(1, 2 * D_MODEL), wspec(D_MODEL, D_MODEL), wspec(1, D_MODEL),
        ],
        out_specs=pl.BlockSpec(memory_space=pl.ANY),
        scratch_shapes=[pltpu.VMEM((MOE_SLOTS, MOE_BLK * ROW_TILE, 128), F32),
                        pltpu.VMEM((MOE_SLOTS, MOE_BLK * ROW_TILE, 128), F32),
                        pltpu.VMEM((D_MODEL, 2 * D_MODEL), _MM), pltpu.VMEM((D_MODEL, D_MODEL), _MM),
                        pltpu.SemaphoreType.DMA((MOE_SLOTS,)), pltpu.SemaphoreType.DMA((MOE_SLOTS,))],
    )
    return pl.pallas_call(
        _moe_kernel,
        grid_spec=grid_spec,
        out_shape=jax.ShapeDtypeStruct((n_out_rows * ROW_TILE, 128), F32),
        compiler_params=_cp("arbitrary"),
        name="moe_experts",
    )(blk_e, nvalid, tok3, tok3, tok3, dst3, x3, w_gu, b_gu[:, :, None, :], w_dn, b_dn[:, :, None, :])


def _combine_kernel(y0_ref, y1_ref, y2_ref, y3_ref, gate_ref, x_ref, g_ref, b_ref, o_ref):
    gate = gate_ref[...]
    ff = gate[:, 0:1] * _load_row_tiles(y0_ref)
    for k, r in enumerate((y1_ref, y2_ref, y3_ref)):
        ff = ff + gate[:, k + 1:k + 2] * _load_row_tiles(r)
    o_ref[...] = _layer_norm(DEEPNORM_ALPHA * x_ref[...] + ff, g_ref[...], b_ref[...])


def _moe_combine(ys, gate_t, x2, g, b):
    n = x2.shape[0]
    steps = n // COMB_T
    const = lambda shape: pl.BlockSpec(shape, lambda i: (0,) * len(shape))
    yspec = lambda k: pl.BlockSpec((COMB_T * ROW_TILE, 128), lambda i: (k * steps + i, 0))
    return pl.pallas_call(
        _combine_kernel,
        grid=(steps,),
        in_specs=[yspec(0), yspec(1), yspec(2), yspec(3), pl.BlockSpec((COMB_T, 8), lambda i: (i, 0)),
                  pl.BlockSpec((COMB_T, D_MODEL), lambda i: (i, 0)), const((1, D_MODEL)), const((1, D_MODEL))],
        out_specs=pl.BlockSpec((COMB_T, D_MODEL), lambda i: (i, 0)),
        out_shape=jax.ShapeDtypeStruct((n, D_MODEL), F32),
        compiler_params=_cp("parallel"),
        name="moe_combine_ln",
    )(ys, ys, ys, ys, gate_t, x2, g.reshape(1, -1), b.reshape(1, -1))


def _moe_ffn(x2, x3, top_i, gate, rank, cnt, layer, w_gu, b_gu, w_dn, b_dn, g, b):
    n = x2.shape[0]
    a = n * TOP_K
    counts = cnt[:, 0].astype(jnp.int32)
    padded = (counts + MOE_BLK - 1) // MOE_BLK * MOE_BLK
    pends = jnp.cumsum(padded)
    pstarts = pends - padded
    n_blocks = a // MOE_BLK + N_EXPERTS
    n_slots = n_blocks * MOE_BLK
    eids = jnp.arange(N_EXPERTS, dtype=jnp.int32)
    hot = top_i[:TOP_K, None, :] == eids[None, :, None]
    dest = jnp.sum(jnp.where(hot, pstarts[None, :, None], 0), axis=1) + rank[:TOP_K]
    spare = a + jnp.arange(n_slots, dtype=jnp.int32) % MOE_BLK
    slot_asg = spare.at[dest.reshape(-1)].set(jnp.arange(a, dtype=jnp.int32))
    slot_tok = jnp.where(slot_asg < a, slot_asg % n, 0)
    slot_dst = jnp.concatenate([spare[:MOE_BLK], slot_asg])
    blk_start = jnp.arange(n_blocks, dtype=jnp.int32) * MOE_BLK
    blk_e = jnp.minimum(jnp.sum((pends[None, :] <= blk_start[:, None]).astype(jnp.int32), axis=1), N_EXPERTS - 1)
    nvalid = (pends[-1:] // MOE_BLK).astype(jnp.int32)
    ys = _moe_experts(x3, slot_tok, slot_dst, blk_e, nvalid, a + MOE_BLK, layer, w_gu, b_gu, w_dn, b_dn)
    return _moe_combine(ys, gate.T, x2, g, b)


def _layer(x, mem, tables, w_in, cmp_pe, cmp_w1, cmp_b1, cmp_w2, conv_w, conv_b, conv_g, conv_beta, gla_w, gla_b,
           gla_g, w_branch, w_out, xa_wq, xa_wkv, xa_wo, router_w, router_b, layer, w_gu, b_gu, w_dn, b_dn, ng, nb):
    B, S, D = x.shape
    n = B * S
    x2 = x.reshape(n, D)
    z2 = _matmul(x2, _reorder_w_in(w_in), 1024, ZW // 5, _ACT)
    z = z2.reshape(B, S, ZW)
    kvc = z[:, :, Z_KV:Z_KV + 256].reshape(B, S // 16, 16, 2, 128).transpose(3, 0, 1, 2, 4)
    cmp = _compress(kvc.reshape(2, B, S // 16, 2048), cmp_pe, cmp_w1, cmp_b1, cmp_w2)
    kv4 = jnp.pad(z[:, :, Z_KV + 256:Z_KV + 768], ((0, 0), (WINDOW, 0), (0, 0)))
    oa = _nsa_attention(z, cmp[0], cmp[1], kv4, tables)
    ob = _conformer_conv(z, conv_w, conv_b, conv_g, conv_beta)
    oc = _gla(z, gla_w, gla_b, gla_g)
    x2 = _merge(oa.reshape(n, 512), ob.reshape(n, 512), oc.reshape(n, 512), z2, w_branch, w_out, x2, ng[0], nb[0])
    kv = _matmul(mem.reshape(-1, D), xa_wkv.astype(_MM), 1024, 1024, _ACT).reshape(B, -1, 2 * D)
    x2, x3, top_i, gate, rank, cnt = _cross_attention_route(x2.reshape(B, S, D), kv, xa_wq, xa_wo, ng[1], nb[1],
                                                            router_w, router_b)
    x2 = _moe_ffn(x2.reshape(n, D), x3, top_i, gate, rank, cnt, layer, w_gu, b_gu, w_dn, b_dn, ng[2], nb[2])
    return x2.reshape(B, S, D)


def kernel(x, mem, rel_bias, w_in, cmp_pe, cmp_w1, cmp_b1, cmp_w2, conv_w, conv_b, conv_norm_g, conv_norm_b,
           gla_gate_w, gla_gate_b, gla_norm_g, w_branch, w_out, xa_wq, xa_wkv, xa_wo, router_w, router_b,
           expert_w_gu, expert_b_gu, expert_w_down, expert_b_down, norm_g, norm_b):
    tables = _nsa_tables(rel_bias)
    for l in range(DEPTH):
        x = _layer(x, mem, tables, w_in[l], cmp_pe[l], cmp_w1[l], cmp_b1[l], cmp_w2[l], conv_w[l], conv_b[l],
                   conv_norm_g[l], conv_norm_b[l], gla_gate_w[l], gla_gate_b[l], gla_norm_g[l], w_branch[l],
                   w_out[l], xa_wq[l], xa_wkv[l], xa_wo[l], router_w[l], router_b[l], l, expert_w_gu,
                   expert_b_gu, expert_w_down, expert_b_down, norm_g[l], norm_b[l])
    return x
```
